```python
import jax
import jax.numpy as jnp
from jax import lax
import numpy as np


D_MODEL = 4096
BATCH = 2
SEQ = 8192
DEPTH = 1

N_META = 16
HEAD_SIZE = 64
D_RWKV = D_MODEL
N_RWKV_HEADS = D_RWKV // HEAD_SIZE
DECAY_LORA = max(32, int(round(1.8 * D_RWKV ** 0.5 / 32)) * 32)
ICLR_LORA = max(32, int(round(1.8 * D_RWKV ** 0.5 / 32)) * 32)
D_CONV = D_MODEL
CONV_WIDTH = 3
D_FF = 4 * D_MODEL
NORM_EPS = 1e-6
GN_EPS = 64e-5
N_RWKV_COLS = 3 * D_RWKV + DECAY_LORA + ICLR_LORA
N_CONV_COLS = 3 * D_CONV
N_IN_COLS = N_RWKV_COLS + N_CONV_COLS + D_RWKV + D_CONV

kernel_name = 'hybrid_rwkv7_shortconv_sqrelu_block'


def rms_norm(x, g):
    xf = x.astype(jnp.float32)
    y = xf * lax.rsqrt(jnp.mean(xf * xf, axis=-1, keepdims=True) + NORM_EPS)
    return (y * g.astype(jnp.float32)).astype(x.dtype)


def token_shift(z):
    return jnp.pad(z, ((0, 0), (1, 0), (0, 0)))[:, :-1]


def rwkv7_scan(r, decay, k, v, kk, a):
    bsz, _, n_heads, n = r.shape

    def step(S, inp):
        r_t, w_t, k_t, v_t, kk_t, a_t = inp
        sa = jnp.einsum('bhij,bhj->bhi', S, -kk_t)
        S = (S * w_t[:, :, None, :]
             + sa[..., None] * (kk_t * a_t)[:, :, None, :]
             + v_t[..., None] * k_t[:, :, None, :])
        y_t = jnp.einsum('bhij,bhj->bhi', S, r_t)
        return S, y_t

    S0 = jnp.zeros((bsz, n_heads, n, n), jnp.float32)
    seq = tuple(jnp.moveaxis(t, 1, 0) for t in (r, decay, k, v, kk, a))
    _, y = lax.scan(step, S0, seq)
    return jnp.moveaxis(y, 0, 1)


def rwkv7_mixer(p, shift_mu, w0, w2, a0, a2, k_k, k_a, r_k, ln_w, ln_b):
    bsz, n_pos, _ = p.shape
    f = lambda t: t.astype(jnp.float32)
    z = f(p + (token_shift(p) - p) * shift_mu)
    r, k, v, zw, za = jnp.split(z, [D_RWKV, 2 * D_RWKV, 3 * D_RWKV, 3 * D_RWKV + DECAY_LORA], axis=-1)
    w_log = -jax.nn.softplus(-(f(w0) + jnp.tanh(zw) @ f(w2))) - 0.5
    decay = jnp.exp(-jnp.exp(w_log))
    a = jax.nn.sigmoid(f(a0) + za @ f(a2))
    heads = lambda t: t.reshape(bsz, n_pos, N_RWKV_HEADS, HEAD_SIZE)
    kk = heads(k * f(k_k))
    kk = kk / jnp.maximum(jnp.sqrt(jnp.sum(kk * kk, axis=-1, keepdims=True)), 1e-12)
    k = k * (1.0 + (a - 1.0) * f(k_a))
    r, k, v, decay, a = map(heads, (r, k, v, decay, a))
    y = rwkv7_scan(r, decay, k, v, kk, a)
    mu = jnp.mean(y, axis=-1, keepdims=True)
    var = jnp.mean(jnp.square(y - mu), axis=-1, keepdims=True)
    y = ((y - mu) * lax.rsqrt(var + GN_EPS)).reshape(bsz, n_pos, D_RWKV) * f(ln_w) + f(ln_b)
    bonus = jnp.sum(r * k * f(r_k), axis=-1, keepdims=True) * v
    return (y + bonus.reshape(bsz, n_pos, D_RWKV)).astype(p.dtype)


def short_conv_mixer(p, conv_w):
    b_gate, c_gate, h = jnp.split(p, [D_CONV, 2 * D_CONV], axis=-1)
    u = c_gate * h
    conv = lax.conv_general_dilated(
        u, conv_w[:, None, :].astype(u.dtype), window_strides=(1,),
        padding=[(CONV_WIDTH - 1, 0)], dimension_numbers=('NWC', 'WIO', 'NWC'),
        feature_group_count=D_CONV)
    return b_gate * conv


def setup_inputs(seed: int = 0) -> dict:
    key = jax.random.key(seed)
    ks = jax.random.split(key, 20)
    f32 = jnp.float32

    def nrm(k, shape, scale):
        return jax.random.normal(k, shape, f32) * scale

    L = DEPTH
    return {
        'x': nrm(ks[0], (BATCH, SEQ, D_MODEL), 1.0),
        'meta_tokens': nrm(ks[1], (N_META, D_MODEL), 1.0),
        'norm_mix_g': 1.0 + nrm(ks[2], (L, D_MODEL), 0.02),
        'w_in': nrm(ks[3], (L, D_MODEL, N_IN_COLS), D_MODEL ** -0.5),
        'rwkv_shift_mu': jax.random.uniform(ks[4], (L, N_RWKV_COLS), f32),
        'rwkv_w0': jax.random.uniform(ks[5], (L, D_RWKV), f32, -6.0, -1.0),
        'rwkv_w2': nrm(ks[6], (L, DECAY_LORA, D_RWKV), 0.5 * DECAY_LORA ** -0.5),
        'rwkv_a0': nrm(ks[7], (L, D_RWKV), 0.1),
        'rwkv_a2': nrm(ks[8], (L, ICLR_LORA, D_RWKV), 0.5 * ICLR_LORA ** -0.5),
        'rwkv_k_k': 0.85 + nrm(ks[9], (L, D_RWKV), 0.02),
        'rwkv_k_a': 1.0 + nrm(ks[10], (L, D_RWKV), 0.02),
        'rwkv_r_k': nrm(ks[11], (L, N_RWKV_HEADS, HEAD_SIZE), 0.1),
        'rwkv_ln_w': 1.0 + nrm(ks[12], (L, D_RWKV), 0.02),
        'rwkv_ln_b': nrm(ks[13], (L, D_RWKV), 0.02),
        'conv_w': nrm(ks[14], (L, CONV_WIDTH, D_CONV), CONV_WIDTH ** -0.5),
        'w_out': nrm(ks[15], (L, D_RWKV + D_CONV, D_MODEL), (D_RWKV + D_CONV) ** -0.5),
        'norm_mlp_g': 1.0 + nrm(ks[16], (L, D_MODEL), 0.02),
        'w_up': nrm(ks[17], (L, D_MODEL, D_FF), D_MODEL ** -0.5),
        'w_down': nrm(ks[18], (L, D_FF, D_MODEL), D_FF ** -0.5),
        'norm_final_g': 1.0 + nrm(ks[19], (D_MODEL,), 0.02),
    }


def reference(x, meta_tokens, norm_mix_g, w_in, rwkv_shift_mu, rwkv_w0, rwkv_w2, rwkv_a0,
              rwkv_a2, rwkv_k_k, rwkv_k_a, rwkv_r_k, rwkv_ln_w, rwkv_ln_b, conv_w, w_out,
              norm_mlp_g, w_up, w_down, norm_final_g):
    bsz = x.shape[0]
    meta = jnp.broadcast_to(meta_tokens.astype(x.dtype)[None], (bsz, N_META, D_MODEL))
    h = jnp.concatenate([meta, x], axis=1)
    split_pts = [N_RWKV_COLS, N_RWKV_COLS + N_CONV_COLS, N_RWKV_COLS + N_CONV_COLS + D_RWKV]
    for layer in range(DEPTH):
        u = rms_norm(h, norm_mix_g[layer])
        p = u @ w_in[layer]
        p_a, p_b, g_a, g_b = jnp.split(p, split_pts, axis=-1)
        y_a = rwkv7_mixer(p_a, rwkv_shift_mu[layer], rwkv_w0[layer], rwkv_w2[layer],
                          rwkv_a0[layer], rwkv_a2[layer], rwkv_k_k[layer], rwkv_k_a[layer],
                          rwkv_r_k[layer], rwkv_ln_w[layer], rwkv_ln_b[layer])
        y_b = short_conv_mixer(p_b, conv_w[layer])
        merged = jnp.concatenate([jax.nn.sigmoid(g_a) * y_a, jax.nn.sigmoid(g_b) * y_b], axis=-1)
        h = h + merged @ w_out[layer]
        u = rms_norm(h, norm_mlp_g[layer])
        h = h + jnp.square(jax.nn.relu(u @ w_up[layer])) @ w_down[layer]
    out = rms_norm(h, norm_final_g)
    return out[:, N_META:]
```

```python
import functools
import math

import jax
import jax.numpy as jnp
from jax import lax
from jax.experimental import pallas as pl
from jax.experimental.pallas import tpu as pltpu

F32 = jnp.float32
BF16 = jnp.bfloat16

HEAD = 64
GROUP = 4
GL = GROUP * HEAD
CHUNK = 64
LORA = 128
NORM_EPS = 1e-6
GN_EPS = 64e-5
LOG_DECAY_SCALE = -math.exp(-0.5)
META_ROWS = 64
VMEM_LIMIT = 56 * 1024 * 1024


def _params(*sem):
    return pltpu.CompilerParams(dimension_semantics=sem, vmem_limit_bytes=VMEM_LIMIT)


def _rmsnorm_kernel(x_ref, g_ref, o_ref):
    x = x_ref[...]
    y = x * lax.rsqrt(jnp.mean(x * x, axis=-1, keepdims=True) + NORM_EPS)
    o_ref[...] = (y * g_ref[...]).astype(o_ref.dtype)


def _rmsnorm(x, g, out_dtype, tm):
    rows, d = x.shape
    return pl.pallas_call(
        _rmsnorm_kernel,
        grid=(rows // tm,),
        in_specs=[pl.BlockSpec((tm, d), lambda i: (i, 0)), pl.BlockSpec((1, d), lambda i: (0, 0))],
        out_specs=pl.BlockSpec((tm, d), lambda i: (i, 0)),
        out_shape=jax.ShapeDtypeStruct((rows, d), out_dtype),
        compiler_params=_params("parallel"),
        name="rmsnorm",
    )(x, g.reshape(1, d))


def _add_rmsnorm_kernel(x_ref, y_ref, g_ref, o_ref):
    x = x_ref[...] + y_ref[...]
    y = x * lax.rsqrt(jnp.mean(x * x, axis=-1, keepdims=True) + NORM_EPS)
    o_ref[...] = (y * g_ref[...]).astype(o_ref.dtype)


def _add_rmsnorm(x, y, g, tm):
    rows, d = x.shape
    return pl.pallas_call(
        _add_rmsnorm_kernel,
        grid=(rows // tm,),
        in_specs=[pl.BlockSpec((tm, d), lambda i: (i, 0)), pl.BlockSpec((tm, d), lambda i: (i, 0)),
                  pl.BlockSpec((1, d), lambda i: (0, 0))],
        out_specs=pl.BlockSpec((tm, d), lambda i: (i, 0)),
        out_shape=jax.ShapeDtypeStruct((rows, d), x.dtype),
        compiler_params=_params("parallel"),
        name="add_rmsnorm",
    )(x, y, g.reshape(1, d))


def _matmul_kernel(x_ref, w_ref, o_ref):
    o_ref[...] = jnp.dot(x_ref[...], w_ref[...], preferred_element_type=F32).astype(o_ref.dtype)


def _matmul(x, w, tm, tn):
    rows, k = x.shape
    n = w.shape[1]
    return pl.pallas_call(
        _matmul_kernel,
        grid=(rows // tm, n // tn),
        in_specs=[pl.BlockSpec((tm, k), lambda i, j: (i, 0)), pl.BlockSpec((k, tn), lambda i, j: (0, j))],
        out_specs=pl.BlockSpec((tm, tn), lambda i, j: (i, j)),
        out_shape=jax.ShapeDtypeStruct((rows, n), F32),
        compiler_params=_params("parallel", "parallel"),
        name="in_proj",
    )(x, w)


def _out_proj_kernel(a_ref, b_ref, wa_ref, wb_ref, h_ref, o_ref):
    acc = jnp.dot(a_ref[...], wa_ref[...], preferred_element_type=F32)
    acc += jnp.dot(b_ref[...], wb_ref[...], preferred_element_type=F32)
    o_ref[...] = h_ref[...] + acc


def _out_proj(ya, yb, wa, wb, h, tm, tn):
    rows, k = ya.shape
    n = wa.shape[1]
    return pl.pallas_call(
        _out_proj_kernel,
        grid=(rows // tm, n // tn),
        in_specs=[pl.BlockSpec((tm, k), lambda i, j: (i, 0)), pl.BlockSpec((tm, k), lambda i, j: (i, 0)),
                  pl.BlockSpec((k, tn), lambda i, j: (0, j)), pl.BlockSpec((k, tn), lambda i, j: (0, j)),
                  pl.BlockSpec((tm, tn), lambda i, j: (i, j))],
        out_specs=pl.BlockSpec((tm, tn), lambda i, j: (i, j)),
        out_shape=jax.ShapeDtypeStruct((rows, n), F32),
        compiler_params=_params("parallel", "parallel"),
        name="out_proj",
    )(ya, yb, wa, wb, h)


def _mlp_kernel(u_ref, wu_ref, wd_ref, o_ref):
    f = pl.program_id(1)
    hid = jnp.dot(u_ref[...], wu_ref[...], preferred_element_type=F32)
    hid = jnp.square(jnp.maximum(hid, 0.0)).astype(BF16)
    part = jnp.dot(hid, wd_ref[...], preferred_element_type=F32)

    @pl.when(f == 0)
    def _():
        o_ref[...] = part

    @pl.when(f > 0)
    def _():
        o_ref[...] += part


def _mlp(u, wu, wd, tm, tf):
    rows, d = u.shape
    dff = wu.shape[1]
    return pl.pallas_call(
        _mlp_kernel,
        grid=(rows // tm, dff // tf),
        in_specs=[pl.BlockSpec((tm, d), lambda i, f: (i, 0)), pl.BlockSpec((d, tf), lambda i, f: (0, f)),
                  pl.BlockSpec((tf, d), lambda i, f: (f, 0))],
        out_specs=pl.BlockSpec((tm, d), lambda i, f: (i, 0)),
        out_shape=jax.ShapeDtypeStruct((rows, d), F32),
        compiler_params=_params("parallel", "arbitrary"),
        name="mlp",
    )(u, wu, wd)


_P_MU_R, _P_MU_K, _P_MU_V, _P_MU_L, _P_W0, _P_A0, _P_KK, _P_KA, _P_RK, _P_LNW, _P_LNB = range(11)
_P_ROWS = 16


def _block_diag_mask(n):
    r = lax.broadcasted_iota(jnp.int32, (n, n), 0) // HEAD
    c = lax.broadcasted_iota(jnp.int32, (n, n), 1) // HEAD
    return r == c


def _rwkv_kernel(pr_ref, pk_ref, pv_ref, pl_ref, pg_ref, par_ref, w2_ref, a2_ref, s0_ref, prev_ref,
                 y_ref, sout_ref,
                 state, carry, r_s, k_s, v_s, kk_s, a_s, lw_s, y_s, *, tile):
    t_idx = pl.program_id(2)

    @pl.when(t_idx == 0)
    def _():
        state[...] = s0_ref[0]
        carry[...] = prev_ref[0]

    par = par_ref[0]
    prow = lambda i: par[i:i + 1, :]
    bd_mask = _block_diag_mask(GL)
    ones_bd = jnp.where(bd_mask, 1.0, 0.0).astype(BF16)
    first_row = lax.broadcasted_iota(jnp.int32, (tile, GL), 0) == 0

    def shift_lerp(p_ref, idx, mu):
        p = p_ref[...]
        prev = jnp.where(first_row, carry[idx:idx + 1, :], pltpu.roll(p, 1, 0))
        carry[idx:idx + 1, :] = p[tile - 1:tile, :]
        return p + (prev - p) * mu

    r = shift_lerp(pr_ref, 0, prow(_P_MU_R))
    k = shift_lerp(pk_ref, 1, prow(_P_MU_K))
    v = shift_lerp(pv_ref, 2, prow(_P_MU_V))
    zl = shift_lerp(pl_ref, 3, prow(_P_MU_L))
    zw = jnp.tanh(zl[:, :LORA]).astype(BF16)
    za = zl[:, LORA:].astype(BF16)
    xw = prow(_P_W0) + jnp.dot(zw, w2_ref[...], preferred_element_type=F32)
    lw = LOG_DECAY_SCALE * jax.nn.sigmoid(xw)
    a = jax.nn.sigmoid(prow(_P_A0) + jnp.dot(za, a2_ref[...], preferred_element_type=F32))
    kk = k * prow(_P_KK)
    ss = jnp.dot((kk * kk).astype(BF16), ones_bd, preferred_element_type=F32)
    kk = kk / jnp.maximum(jnp.sqrt(ss), 1e-12)
    k = k * (1.0 + (a - 1.0) * prow(_P_KA))
    r_s[...] = r
    k_s[...] = k
    v_s[...] = v
    kk_s[...] = kk
    a_s[...] = a
    lw_s[...] = lw

    ti = lax.broadcasted_iota(jnp.int32, (CHUNK, GL), 0)
    si = lax.broadcasted_iota(jnp.int32, (CHUNK, GL), 1) % HEAD
    strict = si < ti
    incl = si <= ti
    eye = jnp.where(si == ti, 1.0, 0.0).astype(F32)
    tri = (lax.broadcasted_iota(jnp.int32, (CHUNK, CHUNK), 1)
           <= lax.broadcasted_iota(jnp.int32, (CHUNK, CHUNK), 0)).astype(BF16)

    def bd(x):
        xb = x.astype(BF16)
        return jnp.where(bd_mask, jnp.concatenate([xb] * GROUP, axis=0), jnp.zeros((), BF16))

    def rob_mm(x, y):
        return jnp.dot(x.astype(BF16), bd(y), preferred_element_type=F32)

    def nt_dot(x, y):
        return lax.dot_general(x, y, (((1,), (1,)), ((), ())), preferred_element_type=F32)

    def chunk_body(c, _):
        off = pl.multiple_of(c * CHUNK, CHUNK)
        rows = pl.ds(off, CHUNK)
        rc, kc, vc, kkc, ac, lwc = r_s[rows, :], k_s[rows, :], v_s[rows, :], kk_s[rows, :], a_s[rows, :], lw_s[rows, :]
        hi = lwc.astype(BF16)
        r1 = lwc - hi.astype(F32)
        mid = r1.astype(BF16)
        lo = (r1 - mid.astype(F32)).astype(BF16)
        cum = (jnp.dot(tri, hi, preferred_element_type=F32) + jnp.dot(tri, mid, preferred_element_type=F32)
               + jnp.dot(tri, lo, preferred_element_type=F32))
        cum_end = cum[CHUNK - 1:CHUNK, :]
        e_cum = jnp.exp(cum)
        e_neg = jnp.exp(-cum)
        e_prev = jnp.exp(cum - lwc)
        e_end = jnp.exp(cum_end - cum)
        kka = kkc * ac
        al = -kkc * e_prev
        rt = rc * e_cum
        bt = kka * e_neg
        kt = kc * e_neg
        bh = kka * e_end
        kh = kc * e_end
        lr = jnp.concatenate([al, rt], axis=0).astype(BF16)
        x_b = nt_dot(lr, bd(bt))
        x_k = nt_dot(lr, bd(kt))
        a_ab = jnp.where(strict, x_b[:CHUNK], 0.0)
        a_rb = jnp.where(incl, x_b[CHUNK:], 0.0)
        a_ak = jnp.where(strict, x_k[:CHUNK], 0.0)
        a_rk = jnp.where(incl, x_k[CHUNK:], 0.0)
        pw = a_ab
        tinv = eye + a_ab
        for _ in range(int(math.log2(CHUNK)) - 1):
            pw = rob_mm(pw, pw)
            tinv = tinv + rob_mm(tinv, pw)
        s_bf = state[...].astype(BF16)
        ars = nt_dot(lr, s_bf)
        bd_v = bd(vc)
        u = rob_mm(tinv, ars[:CHUNK] + jnp.dot(a_ak.astype(BF16), bd_v, preferred_element_type=F32))
        y = ars[CHUNK:] + jnp.dot(jnp.concatenate([a_rb, a_rk], axis=1).astype(BF16),
                                  jnp.concatenate([bd(u), bd_v], axis=0), preferred_element_type=F32)
        y_s[rows, :] = y
        uv = jnp.concatenate([u, vc], axis=0).astype(BF16)
        bk = jnp.concatenate([bh, kh], axis=0).astype(BF16)
        ds = lax.dot_general(uv, bk, (((0,), (0,)), ((), ())), preferred_element_type=F32)
        state[...] = state[...] * jnp.exp(cum_end) + jnp.where(bd_mask, ds, 0.0)
        return 0

    lax.fori_loop(0, tile // CHUNK, chunk_body, 0)

    y = y_s[...]
    inv_n = 1.0 / HEAD
    y_hi = y.astype(BF16)
    y_lo = (y - y_hi.astype(F32)).astype(BF16)
    mu = (jnp.dot(y_hi, ones_bd, preferred_element_type=F32) + jnp.dot(y_lo, ones_bd, preferred_element_type=F32)) * inv_n
    d = y - mu
    var = jnp.dot((d * d).astype(BF16), ones_bd, preferred_element_type=F32) * inv_n
    yn = d * lax.rsqrt(var + GN_EPS) * prow(_P_LNW) + prow(_P_LNB)
    rr, kf, vf = r_s[...], k_s[...], v_s[...]
    bonus = jnp.dot((rr * kf * prow(_P_RK)).astype(BF16), ones_bd, preferred_element_type=F32) * vf
    y_ref[...] = (jax.nn.sigmoid(pg_ref[...]) * (yn + bonus)).astype(y_ref.dtype)

    @pl.when(t_idx == pl.num_programs(2) - 1)
    def _():
        sout_ref[0, 0] = state[...]


def _rwkv(p_rkv, p_lora, p_gate, par, w2, a2, s0, prev, batch, tile):
    rows = p_rkv.shape[0]
    d = p_rkv.shape[1] // 3
    groups = d // GL
    nt = rows // batch // tile
    row_blk = lambda off: pl.BlockSpec((tile, GL), lambda b, g, t: (b * nt + t, off + g))
    scratch = [pltpu.VMEM((GL, GL), F32), pltpu.VMEM((8, GL), F32)] + [pltpu.VMEM((tile, GL), F32)] * 7
    return pl.pallas_call(
        functools.partial(_rwkv_kernel, tile=tile),
        grid=(batch, groups, nt),
        in_specs=[row_blk(0), row_blk(groups), row_blk(2 * groups),
                  pl.BlockSpec((tile, 2 * LORA), lambda b, g, t: (b * nt + t, 0)),
                  row_blk(0),
                  pl.BlockSpec((1, _P_ROWS, GL), lambda b, g, t: (g, 0, 0)),
                  pl.BlockSpec((LORA, GL), lambda b, g, t: (0, g)),
                  pl.BlockSpec((LORA, GL), lambda b, g, t: (0, g)),
                  pl.BlockSpec((1, GL, GL), lambda b, g, t: (g, 0, 0)),
                  pl.BlockSpec((1, 8, GL), lambda b, g, t: (g, 0, 0))],
        out_specs=[row_blk(0), pl.BlockSpec((1, 1, GL, GL), lambda b, g, t: (b, g, 0, 0))],
        out_shape=[jax.ShapeDtypeStruct((rows, d), BF16), jax.ShapeDtypeStruct((batch, groups, GL, GL), F32)],
        scratch_shapes=scratch,
        compiler_params=_params("parallel", "parallel", "arbitrary"),
        name="rwkv7",
    )(p_rkv, p_rkv, p_rkv, p_lora, p_gate, par, w2, a2, s0, prev)


def _conv_kernel(pb_ref, pc_ref, ph_ref, pg_ref, w_ref, mc_ref, mh_ref, o_ref, carry, *, tile, tiles_per_seq):
    i = pl.program_id(1)

    @pl.when(i % tiles_per_seq == 0)
    def _():
        carry[...] = mc_ref[...] * mh_ref[...]

    u = pc_ref[...] * ph_ref[...]
    row = lax.broadcasted_iota(jnp.int32, u.shape, 0)
    prev1 = jnp.where(row == 0, carry[7:8, :], pltpu.roll(u, 1, 0))
    prev2 = jnp.where(row == 0, carry[6:7, :], jnp.where(row == 1, carry[7:8, :], pltpu.roll(u, 2, 0)))
    carry[...] = u[tile - 8:tile, :]
    w = w_ref[...]
    conv = w[0:1, :] * prev2 + w[1:2, :] * prev1 + w[2:3, :] * u
    o_ref[...] = (jax.nn.sigmoid(pg_ref[...]) * (pb_ref[...] * conv)).astype(o_ref.dtype)


def _conv(p_conv, p_gate, gate_off, conv_w, p_conv_meta, batch, tile, tc):
    rows = p_conv.shape[0]
    d = p_conv.shape[1] // 3
    nc = d // tc
    tiles_per_seq = rows // batch // tile
    meta_blk = p_conv_meta.shape[0] // 8 - 1
    blk = lambda off: pl.BlockSpec((tile, tc), lambda j, i: (i, off + j))
    w8 = jnp.zeros((8, d), F32).at[:conv_w.shape[0]].set(conv_w)
    return pl.pallas_call(
        functools.partial(_conv_kernel, tile=tile, tiles_per_seq=tiles_per_seq),
        grid=(nc, rows // tile),
        in_specs=[blk(0), blk(nc), blk(2 * nc), blk(gate_off // tc),
                  pl.BlockSpec((8, tc), lambda j, i: (0, j)),
                  pl.BlockSpec((8, tc), lambda j, i: (meta_blk, nc + j)),
                  pl.BlockSpec((8, tc), lambda j, i: (meta_blk, 2 * nc + j))],
        out_specs=blk(0),
        out_shape=jax.ShapeDtypeStruct((rows, d), BF16),
        scratch_shapes=[pltpu.VMEM((8, tc), F32)],
        compiler_params=_params("parallel", "arbitrary"),
        name="short_conv",
    )(p_conv, p_conv, p_conv, p_gate, w8, p_conv_meta, p_conv_meta)


def _pick(n, prefs):
    for p in prefs:
        if n % p == 0:
            return p
    raise ValueError(f"no tile for {n} in {prefs}")


def kernel(x, meta_tokens, norm_mix_g, w_in, rwkv_shift_mu, rwkv_w0, rwkv_w2, rwkv_a0, rwkv_a2, rwkv_k_k, rwkv_k_a, rwkv_r_k, rwkv_ln_w, rwkv_ln_b, conv_w, w_out, norm_mlp_g, w_up, w_down, norm_final_g):
    bsz, seq, d = x.shape
    n_meta = meta_tokens.shape[0]
    depth = w_in.shape[0]
    assert depth == 1, "the meta-token hand-off is written for a single layer"
    assert d % GL == 0 and seq % CHUNK == 0 and n_meta <= META_ROWS
    groups = d // GL
    rows = bsz * seq
    layer = 0

    c_rkv, c_lora = 3 * d, 2 * LORA
    c0 = c_rkv + c_lora
    w_all = w_in[layer]
    w_rkv = w_all[:, :c_rkv].astype(BF16)
    w_lora = w_all[:, c_rkv:c0].astype(BF16)
    w_conv = w_all[:, c0:c0 + 3 * d].astype(BF16)
    w_gate = w_all[:, c0 + 3 * d:].astype(BF16)
    wo_a = w_out[layer, :d].astype(BF16)
    wo_b = w_out[layer, d:].astype(BF16)
    wu = w_up[layer].astype(BF16)
    wd = w_down[layer].astype(BF16)
    w2 = rwkv_w2[layer].astype(BF16)
    a2 = rwkv_a2[layer].astype(BF16)

    mu = rwkv_shift_mu[layer]
    per_group = lambda vec: vec.reshape(groups, 1, GL)
    par = jnp.concatenate(
        [per_group(mu[:d]), per_group(mu[d:2 * d]), per_group(mu[2 * d:3 * d]),
         jnp.broadcast_to(mu[c_rkv:c0].reshape(1, 1, GL), (groups, 1, GL)),
         per_group(rwkv_w0[layer]), per_group(rwkv_a0[layer]), per_group(rwkv_k_k[layer]),
         per_group(rwkv_k_a[layer]), per_group(rwkv_r_k[layer].reshape(d)),
         per_group(rwkv_ln_w[layer]), per_group(rwkv_ln_b[layer]),
         jnp.zeros((groups, _P_ROWS - 11, GL), F32)], axis=1)

    def in_proj(h_rows, tm):
        u = _rmsnorm(h_rows, norm_mix_g[layer], BF16, min(tm, 256))
        mm = lambda w: _matmul(u, w, tm, _pick(w.shape[1], (1024, 512, 256)))
        return mm(w_rkv), mm(w_lora), mm(w_conv), mm(w_gate)

    meta = jnp.zeros((META_ROWS, d), F32).at[META_ROWS - n_meta:].set(meta_tokens.astype(F32))
    m_rkv, m_lora, m_conv, m_gate = in_proj(meta, META_ROWS)
    _, s_meta = _rwkv(m_rkv, m_lora, m_gate, par, w2, a2, jnp.zeros((groups, GL, GL), F32),
                      jnp.zeros((groups, 8, GL), F32), 1, META_ROWS)
    last = jnp.concatenate([m_rkv[-1].reshape(3, groups, GL).transpose(1, 0, 2),
                            jnp.broadcast_to(m_lora[-1].reshape(1, 1, GL), (groups, 1, GL)),
                            jnp.zeros((groups, 4, GL), F32)], axis=1)

    xf = x.reshape(rows, d)
    tm = _pick(rows, (1024, 512, 256, 128, 64))
    p_rkv, p_lora, p_conv, p_gate = in_proj(xf, tm)
    t_scan = _pick(seq, (512, 256, 128, 64))
    ya, _ = _rwkv(p_rkv, p_lora, p_gate, par, w2, a2, s_meta[0], last, bsz, t_scan)
    t_conv = _pick(seq, (512, 256, 128, 64))
    yb = _conv(p_conv, p_gate, d, conv_w[layer], m_conv, bsz, t_conv, _pick(d, (512, 256, 128)))
    tq = _pick(rows, (512, 256, 128, 64))
    h1 = _out_proj(ya, yb, wo_a, wo_b, xf, tq, _pick(d, (512, 256, 128)))
    u2 = _rmsnorm(h1, norm_mlp_g[layer], BF16, min(tq, 256))
    m = _mlp(u2, wu, wd, tq, _pick(wu.shape[1], (512, 256, 128)))
    out = _add_rmsnorm(h1, m, norm_final_g, min(tq, 256))
    return out.reshape(bsz, seq, d)
```

```python
import functools
import math

import jax
import jax.numpy as jnp
from jax import lax
from jax.experimental import pallas as pl
from jax.experimental.pallas import tpu as pltpu

F32 = jnp.float32
BF16 = jnp.bfloat16

HEAD = 64
GROUP = 4
GL = GROUP * HEAD
CHUNK = 64
LORA = 128
NORM_EPS = 1e-6
GN_EPS = 64e-5
LOG_DECAY_SCALE = -math.exp(-0.5)
META_ROWS = 64
VMEM_LIMIT = 56 * 1024 * 1024


def _params(*sem):
    return pltpu.CompilerParams(dimension_semantics=sem, vmem_limit_bytes=VMEM_LIMIT)


def _rmsnorm_kernel(x_ref, g_ref, o_ref):
    x = x_ref[...]
    y = x * lax.rsqrt(jnp.mean(x * x, axis=-1, keepdims=True) + NORM_EPS)
    o_ref[...] = (y * g_ref[...]).astype(o_ref.dtype)


def _rmsnorm(x, g, out_dtype, tm):
    rows, d = x.shape
    return pl.pallas_call(
        _rmsnorm_kernel,
        grid=(rows // tm,),
        in_specs=[pl.BlockSpec((tm, d), lambda i: (i, 0)), pl.BlockSpec((1, d), lambda i: (0, 0))],
        out_specs=pl.BlockSpec((tm, d), lambda i: (i, 0)),
        out_shape=jax.ShapeDtypeStruct((rows, d), out_dtype),
        compiler_params=_params("parallel"),
        name="rmsnorm",
    )(x, g.reshape(1, d))


def _matmul_kernel(x_ref, w_ref, o_ref):
    o_ref[...] = jnp.dot(x_ref[...], w_ref[...], preferred_element_type=F32).astype(o_ref.dtype)


def _matmul(x, w, tm, tn):
    rows, k = x.shape
    n = w.shape[1]
    return pl.pallas_call(
        _matmul_kernel,
        grid=(rows // tm, n // tn),
        in_specs=[pl.BlockSpec((tm, k), lambda i, j: (i, 0)), pl.BlockSpec((k, tn), lambda i, j: (0, j))],
        out_specs=pl.BlockSpec((tm, tn), lambda i, j: (i, j)),
        out_shape=jax.ShapeDtypeStruct((rows, n), F32),
        compiler_params=_params("parallel", "parallel"),
        name="in_proj",
    )(x, w)


def _out_proj_kernel(a_ref, b_ref, wa_ref, wb_ref, h_ref, o_ref):
    acc = jnp.dot(a_ref[...], wa_ref[...], preferred_element_type=F32)
    acc += jnp.dot(b_ref[...], wb_ref[...], preferred_element_type=F32)
    o_ref[...] = h_ref[...] + acc


def _out_proj(ya, yb, wa, wb, h, tm, tn):
    rows, k = ya.shape
    n = wa.shape[1]
    return pl.pallas_call(
        _out_proj_kernel,
        grid=(rows // tm, n // tn),
        in_specs=[pl.BlockSpec((tm, k), lambda i, j: (i, 0)), pl.BlockSpec((tm, k), lambda i, j: (i, 0)),
                  pl.BlockSpec((k, tn), lambda i, j: (0, j)), pl.BlockSpec((k, tn), lambda i, j: (0, j)),
                  pl.BlockSpec((tm, tn), lambda i, j: (i, j))],
        out_specs=pl.BlockSpec((tm, tn), lambda i, j: (i, j)),
        out_shape=jax.ShapeDtypeStruct((rows, n), F32),
        compiler_params=_params("parallel", "parallel"),
        name="out_proj",
    )(ya, yb, wa, wb, h)


def _mlp_up_kernel(x_ref, w_ref, o_ref):
    hid = jnp.dot(x_ref[...], w_ref[...], preferred_element_type=F32)
    o_ref[...] = jnp.square(jnp.maximum(hid, 0.0)).astype(o_ref.dtype)


def _mlp_up(x, w, tm, tn):
    rows, k = x.shape
    n = w.shape[1]
    return pl.pallas_call(
        _mlp_up_kernel,
        grid=(rows // tm, n // tn),
        in_specs=[pl.BlockSpec((tm, k), lambda i, j: (i, 0)), pl.BlockSpec((k, tn), lambda i, j: (0, j))],
        out_specs=pl.BlockSpec((tm, tn), lambda i, j: (i, j)),
        out_shape=jax.ShapeDtypeStruct((rows, n), BF16),
        compiler_params=_params("parallel", "parallel"),
        name="mlp_up",
    )(x, w)


def _mlp_down_kernel(x_ref, w_ref, h_ref, o_ref):
    part = jnp.dot(x_ref[...], w_ref[...], preferred_element_type=F32)

    @pl.when(pl.program_id(2) == 0)
    def _():
        o_ref[...] = h_ref[...] + part

    @pl.when(pl.program_id(2) > 0)
    def _():
        o_ref[...] += part


def _mlp_down(x, w, h, tm, tn, tk):
    rows, k = x.shape
    n = w.shape[1]
    return pl.pallas_call(
        _mlp_down_kernel,
        grid=(rows // tm, n // tn, k // tk),
        in_specs=[pl.BlockSpec((tm, tk), lambda i, j, q: (i, q)), pl.BlockSpec((tk, tn), lambda i, j, q: (q, j)),
                  pl.BlockSpec((tm, tn), lambda i, j, q: (i, j))],
        out_specs=pl.BlockSpec((tm, tn), lambda i, j, q: (i, j)),
        out_shape=jax.ShapeDtypeStruct((rows, n), F32),
        compiler_params=_params("parallel", "parallel", "arbitrary"),
        name="mlp_down",
    )(x, w, h)


_P_MU_R, _P_MU_K, _P_MU_V, _P_MU_L, _P_W0, _P_A0, _P_KK, _P_KA, _P_RK, _P_LNW, _P_LNB = range(11)
_P_ROWS = 16


def _rwkv_kernel(pr_ref, pk_ref, pv_ref, pl_ref, pg_ref, par_ref, w2_ref, a2_ref, s0_ref, prev_ref,
                 y_ref, sout_ref,
                 state, carry, r_s, k_s, v_s, kk_s, a_s, lw_s, y_s, *, tile, ng):
    t_idx = pl.program_id(2)

    @pl.when(t_idx == 0)
    def _():
        state[...] = s0_ref[...]
        carry[...] = prev_ref[0]

    par = par_ref[0]
    prow = lambda i: par[i:i + 1, :]
    ones_bd = (lax.broadcasted_iota(jnp.int32, (GL, GL), 0) // HEAD
               == lax.broadcasted_iota(jnp.int32, (GL, GL), 1) // HEAD).astype(BF16)
    first_row = lax.broadcasted_iota(jnp.int32, (tile, ng * GL), 0) == 0

    def head_sum(xb):
        return jnp.concatenate([jnp.dot(xb[:, g * GL:(g + 1) * GL], ones_bd, preferred_element_type=F32)
                                for g in range(ng)], axis=1)

    def shift_lerp(p_ref, idx, mu, first):
        p = p_ref[...]
        prev = jnp.where(first, carry[idx:idx + 1, :p.shape[1]], pltpu.roll(p, 1, 0))
        carry[idx:idx + 1, :p.shape[1]] = p[tile - 1:tile, :]
        return p + (prev - p) * mu

    r = shift_lerp(pr_ref, 0, prow(_P_MU_R), first_row)
    k = shift_lerp(pk_ref, 1, prow(_P_MU_K), first_row)
    v = shift_lerp(pv_ref, 2, prow(_P_MU_V), first_row)
    first_row_l = lax.broadcasted_iota(jnp.int32, (tile, 2 * LORA), 0) == 0
    zl = shift_lerp(pl_ref, 3, par[_P_MU_L:_P_MU_L + 1, :2 * LORA], first_row_l)
    zw = jnp.tanh(zl[:, :LORA]).astype(BF16)
    za = zl[:, LORA:].astype(BF16)
    xw = prow(_P_W0) + jnp.dot(zw, w2_ref[...], preferred_element_type=F32)
    lw = LOG_DECAY_SCALE * jax.nn.sigmoid(xw)
    a = jax.nn.sigmoid(prow(_P_A0) + jnp.dot(za, a2_ref[...], preferred_element_type=F32))
    kk = k * prow(_P_KK)
    kk = kk / jnp.maximum(jnp.sqrt(head_sum((kk * kk).astype(BF16))), 1e-12)
    k = k * (1.0 + (a - 1.0) * prow(_P_KA))
    r_s[...] = r
    k_s[...] = k
    v_s[...] = v
    kk_s[...] = kk
    a_s[...] = a
    lw_s[...] = lw

    ti = lax.broadcasted_iota(jnp.int32, (CHUNK, GL), 0)
    lane = lax.broadcasted_iota(jnp.int32, (CHUNK, GL), 1)
    si = lane % HEAD
    strict = si < ti
    incl = si <= ti
    eye = jnp.where(si == ti, 1.0, 0.0).astype(F32)
    head_of_lane = [lane // HEAD == h for h in range(GROUP)]
    bd_mask = (lax.broadcasted_iota(jnp.int32, (GL, GL), 0) // HEAD
               == lax.broadcasted_iota(jnp.int32, (GL, GL), 1) // HEAD)
    tri = (lax.broadcasted_iota(jnp.int32, (CHUNK, CHUNK), 1)
           <= lax.broadcasted_iota(jnp.int32, (CHUNK, CHUNK), 0)).astype(BF16)

    def bd(x):
        xb = x.astype(BF16)
        zero = jnp.zeros((), BF16)
        return jnp.concatenate([jnp.where(m, xb, zero) for m in head_of_lane], axis=0)

    def rob_mm(x, y):
        return jnp.dot(x.astype(BF16), bd(y), preferred_element_type=F32)

    def nt_dot(x, y):
        return lax.dot_general(x, y, (((1,), (1,)), ((), ())), preferred_element_type=F32)

    def group_chunk(rows, g):
        cols = slice(g * GL, (g + 1) * GL)
        rc, kc, vc, kkc, ac, lwc = (s[rows, cols] for s in (r_s, k_s, v_s, kk_s, a_s, lw_s))
        hi = lwc.astype(BF16)
        r1 = lwc - hi.astype(F32)
        mid = r1.astype(BF16)
        lo = (r1 - mid.astype(F32)).astype(BF16)
        cum = (jnp.dot(tri, hi, preferred_element_type=F32) + jnp.dot(tri, mid, preferred_element_type=F32)
               + jnp.dot(tri, lo, preferred_element_type=F32))
        yield
        cum_end = cum[CHUNK - 1:CHUNK, :]
        e_cum = jnp.exp(cum)
        e_neg = jnp.exp(-cum)
        e_prev = jnp.exp(cum - lwc)
        e_end = jnp.exp(cum_end - cum)
        kka = kkc * ac
        al = -kkc * e_prev
        rt = rc * e_cum
        bt = kka * e_neg
        kt = kc * e_neg
        bh = kka * e_end
        kh = kc * e_end
        lr = jnp.concatenate([al, rt], axis=0).astype(BF16)
        x_b = nt_dot(lr, bd(bt))
        x_k = nt_dot(lr, bd(kt))
        s_old = state[g]
        ars = nt_dot(lr, s_old.astype(BF16))
        yield
        a_ab = jnp.where(strict, x_b[:CHUNK], 0.0)
        a_rb = jnp.where(incl, x_b[CHUNK:], 0.0)
        a_ak = jnp.where(strict, x_k[:CHUNK], 0.0)
        a_rk = jnp.where(incl, x_k[CHUNK:], 0.0)
        bd_v = bd(vc)
        rhs = ars[:CHUNK] + jnp.dot(a_ak.astype(BF16), bd_v, preferred_element_type=F32)
        pw = a_ab
        tinv = eye + a_ab
        for _ in range(int(math.log2(CHUNK)) - 1):
            pw = rob_mm(pw, pw)
            yield
            tinv = tinv + rob_mm(tinv, pw)
            yield
        u = rob_mm(tinv, rhs)
        yield
        y = ars[CHUNK:] + jnp.dot(jnp.concatenate([a_rb, a_rk], axis=1).astype(BF16),
                                  jnp.concatenate([bd(u), bd_v], axis=0), preferred_element_type=F32)
        y_s[rows, cols] = y
        uv = jnp.concatenate([u, vc], axis=0).astype(BF16)
        bk = jnp.concatenate([bh, kh], axis=0).astype(BF16)
        ds = lax.dot_general(uv, bk, (((0,), (0,)), ((), ())), preferred_element_type=F32)
        state[g] = s_old * jnp.exp(cum_end) + jnp.where(bd_mask, ds, 0.0)

    def chunk_body(c, _):
        rows = pl.ds(pl.multiple_of(c * CHUNK, CHUNK), CHUNK)
        live = [group_chunk(rows, g) for g in range(ng)]
        while live:
            live = [gen for gen in live if next(gen, StopIteration) is not StopIteration]
        return 0

    lax.fori_loop(0, tile // CHUNK, chunk_body, 0)

    y = y_s[...]
    inv_n = 1.0 / HEAD
    y_hi = y.astype(BF16)
    y_lo = (y - y_hi.astype(F32)).astype(BF16)
    d = y - (head_sum(y_hi) + head_sum(y_lo)) * inv_n
    var = head_sum((d * d).astype(BF16)) * inv_n
    yn = d * lax.rsqrt(var + GN_EPS) * prow(_P_LNW) + prow(_P_LNB)
    bonus = head_sum((r_s[...] * k_s[...] * prow(_P_RK)).astype(BF16)) * v_s[...]
    y_ref[...] = (jax.nn.sigmoid(pg_ref[...]) * (yn + bonus)).astype(y_ref.dtype)

    @pl.when(t_idx == pl.num_programs(2) - 1)
    def _():
        sout_ref[0] = state[...]


def _rwkv(p_rkv, p_lora, p_gate, par, w2, a2, s0, prev, batch, tile, ng):
    rows = p_rkv.shape[0]
    d = p_rkv.shape[1] // 3
    groups = d // GL
    sg = groups // ng
    width = ng * GL
    nt = rows // batch // tile
    wide = lambda arr: arr.reshape(sg, ng, arr.shape[1], GL).transpose(0, 2, 1, 3).reshape(sg, arr.shape[1], width)
    row_blk = lambda off: pl.BlockSpec((tile, width), lambda b, g, t: (b * nt + t, off + g))
    scratch = ([pltpu.VMEM((ng, GL, GL), F32), pltpu.VMEM((8, width), F32)]
               + [pltpu.VMEM((tile, width), F32)] * 7)
    return pl.pallas_call(
        functools.partial(_rwkv_kernel, tile=tile, ng=ng),
        grid=(batch, sg, nt),
        in_specs=[row_blk(0), row_blk(sg), row_blk(2 * sg),
                  pl.BlockSpec((tile, 2 * LORA), lambda b, g, t: (b * nt + t, 0)),
                  row_blk(0),
                  pl.BlockSpec((1, _P_ROWS, width), lambda b, g, t: (g, 0, 0)),
                  pl.BlockSpec((LORA, width), lambda b, g, t: (0, g)),
                  pl.BlockSpec((LORA, width), lambda b, g, t: (0, g)),
                  pl.BlockSpec((ng, GL, GL), lambda b, g, t: (g, 0, 0)),
                  pl.BlockSpec((1, 8, width), lambda b, g, t: (g, 0, 0))],
        out_specs=[row_blk(0), pl.BlockSpec((1, ng, GL, GL), lambda b, g, t: (b * sg + g, 0, 0, 0))],
        out_shape=[jax.ShapeDtypeStruct((rows, d), BF16), jax.ShapeDtypeStruct((batch * sg, ng, GL, GL), F32)],
        scratch_shapes=scratch,
        compiler_params=_params("parallel", "parallel", "arbitrary"),
        name="rwkv7",
    )(p_rkv, p_rkv, p_rkv, p_lora, p_gate, wide(par), w2, a2, s0, wide(prev))


def _conv_kernel(pb_ref, pc_ref, ph_ref, pg_ref, w_ref, mc_ref, mh_ref, o_ref, carry, *, tile, tiles_per_seq):
    i = pl.program_id(1)

    @pl.when(i % tiles_per_seq == 0)
    def _():
        carry[...] = mc_ref[...] * mh_ref[...]

    u = pc_ref[...] * ph_ref[...]
    row = lax.broadcasted_iota(jnp.int32, u.shape, 0)
    prev1 = jnp.where(row == 0, carry[7:8, :], pltpu.roll(u, 1, 0))
    prev2 = jnp.where(row == 0, carry[6:7, :], jnp.where(row == 1, carry[7:8, :], pltpu.roll(u, 2, 0)))
    carry[...] = u[tile - 8:tile, :]
    w = w_ref[...]
    conv = w[0:1, :] * prev2 + w[1:2, :] * prev1 + w[2:3, :] * u
    o_ref[...] = (jax.nn.sigmoid(pg_ref[...]) * (pb_ref[...] * conv)).astype(o_ref.dtype)


def _conv(p_conv, p_gate, gate_off, conv_w, p_conv_meta, batch, tile, tc):
    rows = p_conv.shape[0]
    d = p_conv.shape[1] // 3
    nc = d // tc
    tiles_per_seq = rows // batch // tile
    meta_blk = p_conv_meta.shape[0] // 8 - 1
    blk = lambda off: pl.BlockSpec((tile, tc), lambda j, i: (i, off + j))
    w8 = jnp.zeros((8, d), F32).at[:conv_w.shape[0]].set(conv_w)
    return pl.pallas_call(
        functools.partial(_conv_kernel, tile=tile, tiles_per_seq=tiles_per_seq),
        grid=(nc, rows // tile),
        in_specs=[blk(0), blk(nc), blk(2 * nc), blk(gate_off // tc),
                  pl.BlockSpec((8, tc), lambda j, i: (0, j)),
                  pl.BlockSpec((8, tc), lambda j, i: (meta_blk, nc + j)),
                  pl.BlockSpec((8, tc), lambda j, i: (meta_blk, 2 * nc + j))],
        out_specs=blk(0),
        out_shape=jax.ShapeDtypeStruct((rows, d), BF16),
        scratch_shapes=[pltpu.VMEM((8, tc), F32)],
        compiler_params=_params("parallel", "arbitrary"),
        name="short_conv",
    )(p_conv, p_conv, p_conv, p_gate, w8, p_conv_meta, p_conv_meta)


def _pick(n, prefs):
    for p in prefs:
        if n % p == 0:
            return p
    raise ValueError(f"no tile for {n} in {prefs}")


def kernel(x, meta_tokens, norm_mix_g, w_in, rwkv_shift_mu, rwkv_w0, rwkv_w2, rwkv_a0, rwkv_a2, rwkv_k_k, rwkv_k_a, rwkv_r_k, rwkv_ln_w, rwkv_ln_b, conv_w, w_out, norm_mlp_g, w_up, w_down, norm_final_g):
    bsz, seq, d = x.shape
    n_meta = meta_tokens.shape[0]
    depth = w_in.shape[0]
    assert depth == 1, "the meta-token hand-off is written for a single layer"
    assert d % GL == 0 and seq % CHUNK == 0 and n_meta <= META_ROWS
    groups = d // GL
    rows = bsz * seq
    layer = 0

    c_rkv, c_lora = 3 * d, 2 * LORA
    c0 = c_rkv + c_lora
    w_all = w_in[layer]
    w_rkv = w_all[:, :c_rkv].astype(BF16)
    w_lora = w_all[:, c_rkv:c0].astype(BF16)
    w_conv = w_all[:, c0:c0 + 3 * d].astype(BF16)
    w_gate = w_all[:, c0 + 3 * d:].astype(BF16)
    wo_a = w_out[layer, :d].astype(BF16)
    wo_b = w_out[layer, d:].astype(BF16)
    wu = w_up[layer].astype(BF16)
    wd = w_down[layer].astype(BF16)
    w2 = rwkv_w2[layer].astype(BF16)
    a2 = rwkv_a2[layer].astype(BF16)

    mu = rwkv_shift_mu[layer]
    per_group = lambda vec: vec.reshape(groups, 1, GL)
    par = jnp.concatenate(
        [per_group(mu[:d]), per_group(mu[d:2 * d]), per_group(mu[2 * d:3 * d]),
         jnp.broadcast_to(mu[c_rkv:c0].reshape(1, 1, GL), (groups, 1, GL)),
         per_group(rwkv_w0[layer]), per_group(rwkv_a0[layer]), per_group(rwkv_k_k[layer]),
         per_group(rwkv_k_a[layer]), per_group(rwkv_r_k[layer].reshape(d)),
         per_group(rwkv_ln_w[layer]), per_group(rwkv_ln_b[layer]),
         jnp.zeros((groups, _P_ROWS - 11, GL), F32)], axis=1)

    def in_proj(h_rows, tm):
        u = _rmsnorm(h_rows, norm_mix_g[layer], BF16, min(tm, 256))
        mm = lambda w: _matmul(u, w, tm, _pick(w.shape[1], (1024, 512, 256)))
        return mm(w_rkv), mm(w_lora), mm(w_conv), mm(w_gate)

    meta = jnp.zeros((META_ROWS, d), F32).at[META_ROWS - n_meta:].set(meta_tokens.astype(F32))
    m_rkv, m_lora, m_conv, m_gate = in_proj(meta, META_ROWS)
    ng = _pick(groups, (8, 4, 2, 1))
    _, s_meta = _rwkv(m_rkv, m_lora, m_gate, par, w2, a2, jnp.zeros((groups, GL, GL), F32),
                      jnp.zeros((groups, 8, GL), F32), 1, META_ROWS, ng)
    last = jnp.concatenate([m_rkv[-1].reshape(3, groups, GL).transpose(1, 0, 2),
                            jnp.broadcast_to(m_lora[-1].reshape(1, 1, GL), (groups, 1, GL)),
                            jnp.zeros((groups, 4, GL), F32)], axis=1)

    xf = x.reshape(rows, d)
    tm = _pick(rows, (1024, 512, 256, 128, 64))
    p_rkv, p_lora, p_conv, p_gate = in_proj(xf, tm)
    t_scan = _pick(seq, (256, 128, 64))
    ya, _ = _rwkv(p_rkv, p_lora, p_gate, par, w2, a2, s_meta.reshape(groups, GL, GL), last, bsz, t_scan, ng)
    t_conv = _pick(seq, (512, 256, 128, 64))
    yb = _conv(p_conv, p_gate, d, conv_w[layer], m_conv, bsz, t_conv, _pick(d, (512, 256, 128)))
    tq = _pick(rows, (512, 256, 128, 64))
    h1 = _out_proj(ya, yb, wo_a, wo_b, xf, tq, _pick(d, (512, 256, 128)))
    u2 = _rmsnorm(h1, norm_mlp_g[layer], BF16, min(tq, 256))
    big = (1024, 512, 256, 128)
    hid = _mlp_up(u2, wu, tm, _pick(wu.shape[1], big))
    h2 = _mlp_down(hid, wd, h1, tm, _pick(d, big), _pick(wu.shape[1], (2048, 1024, 512)))
    out = _rmsnorm(h2, norm_final_g, F32, min(tq, 256))
    return out.reshape(bsz, seq, d)
```

```python
import functools
import math

import jax
import jax.numpy as jnp
from jax import lax
from jax.experimental import pallas as pl
from jax.experimental.pallas import tpu as pltpu

F32 = jnp.float32
BF16 = jnp.bfloat16

HEAD = 64
GROUP = 4
GL = GROUP * HEAD
CHUNK = 64
LORA = 128
NORM_EPS = 1e-6
GN_EPS = 64e-5
LOG2_DECAY_SCALE = -math.exp(-0.5) / math.log(2.0)
META_ROWS = 64
VMEM_LIMIT = 56 * 1024 * 1024


def _params(*sem):
    return pltpu.CompilerParams(dimension_semantics=sem, vmem_limit_bytes=VMEM_LIMIT)


def _sigmoid(x):
    return 0.5 * jnp.tanh(0.5 * x) + 0.5


def _rmsnorm_kernel(x_ref, g_ref, o_ref):
    x = x_ref[...]
    y = x * lax.rsqrt(jnp.mean(x * x, axis=-1, keepdims=True) + NORM_EPS)
    o_ref[...] = (y * g_ref[...]).astype(o_ref.dtype)


def _rmsnorm(x, g, out_dtype, tm):
    rows, d = x.shape
    return pl.pallas_call(
        _rmsnorm_kernel,
        grid=(rows // tm,),
        in_specs=[pl.BlockSpec((tm, d), lambda i: (i, 0)), pl.BlockSpec((1, d), lambda i: (0, 0))],
        out_specs=pl.BlockSpec((tm, d), lambda i: (i, 0)),
        out_shape=jax.ShapeDtypeStruct((rows, d), out_dtype),
        compiler_params=_params("parallel"),
        name="rmsnorm",
    )(x, g.reshape(1, d))


def _matmul_kernel(x_ref, w_ref, o_ref):
    o_ref[...] = jnp.dot(x_ref[...], w_ref[...], preferred_element_type=F32).astype(o_ref.dtype)


def _matmul(x, w, tm, tn):
    rows, k = x.shape
    n = w.shape[1]
    return pl.pallas_call(
        _matmul_kernel,
        grid=(rows // tm, n // tn),
        in_specs=[pl.BlockSpec((tm, k), lambda i, j: (i, 0)), pl.BlockSpec((k, tn), lambda i, j: (0, j))],
        out_specs=pl.BlockSpec((tm, tn), lambda i, j: (i, j)),
        out_shape=jax.ShapeDtypeStruct((rows, n), F32),
        compiler_params=_params("parallel", "parallel"),
        name="in_proj",
    )(x, w)


def _out_proj_kernel(a_ref, b_ref, wa_ref, wb_ref, h_ref, o_ref):
    acc = jnp.dot(a_ref[...], wa_ref[...], preferred_element_type=F32)
    acc += jnp.dot(b_ref[...], wb_ref[...], preferred_element_type=F32)
    o_ref[...] = h_ref[...] + acc


def _out_proj(ya, yb, wa, wb, h, tm, tn):
    rows, k = ya.shape
    n = wa.shape[1]
    return pl.pallas_call(
        _out_proj_kernel,
        grid=(rows // tm, n // tn),
        in_specs=[pl.BlockSpec((tm, k), lambda i, j: (i, 0)), pl.BlockSpec((tm, k), lambda i, j: (i, 0)),
                  pl.BlockSpec((k, tn), lambda i, j: (0, j)), pl.BlockSpec((k, tn), lambda i, j: (0, j)),
                  pl.BlockSpec((tm, tn), lambda i, j: (i, j))],
        out_specs=pl.BlockSpec((tm, tn), lambda i, j: (i, j)),
        out_shape=jax.ShapeDtypeStruct((rows, n), F32),
        compiler_params=_params("parallel", "parallel"),
        name="out_proj",
    )(ya, yb, wa, wb, h)


def _mlp_up_kernel(x_ref, w_ref, o_ref):
    hid = jnp.dot(x_ref[...], w_ref[...], preferred_element_type=F32)
    o_ref[...] = jnp.square(jnp.maximum(hid, 0.0)).astype(o_ref.dtype)


def _mlp_up(x, w, tm, tn):
    rows, k = x.shape
    n = w.shape[1]
    return pl.pallas_call(
        _mlp_up_kernel,
        grid=(rows // tm, n // tn),
        in_specs=[pl.BlockSpec((tm, k), lambda i, j: (i, 0)), pl.BlockSpec((k, tn), lambda i, j: (0, j))],
        out_specs=pl.BlockSpec((tm, tn), lambda i, j: (i, j)),
        out_shape=jax.ShapeDtypeStruct((rows, n), BF16),
        compiler_params=_params("parallel", "parallel"),
        name="mlp_up",
    )(x, w)


def _mlp_down_kernel(x_ref, w_ref, h_ref, o_ref):
    part = jnp.dot(x_ref[...], w_ref[...], preferred_element_type=F32)

    @pl.when(pl.program_id(2) == 0)
    def _():
        o_ref[...] = h_ref[...] + part

    @pl.when(pl.program_id(2) > 0)
    def _():
        o_ref[...] += part


def _mlp_down(x, w, h, tm, tn, tk):
    rows, k = x.shape
    n = w.shape[1]
    return pl.pallas_call(
        _mlp_down_kernel,
        grid=(rows // tm, n // tn, k // tk),
        in_specs=[pl.BlockSpec((tm, tk), lambda i, j, q: (i, q)), pl.BlockSpec((tk, tn), lambda i, j, q: (q, j)),
                  pl.BlockSpec((tm, tn), lambda i, j, q: (i, j))],
        out_specs=pl.BlockSpec((tm, tn), lambda i, j, q: (i, j)),
        out_shape=jax.ShapeDtypeStruct((rows, n), F32),
        compiler_params=_params("parallel", "parallel", "arbitrary"),
        name="mlp_down",
    )(x, w, h)


_P_MU_R, _P_MU_K, _P_MU_V, _P_MU_L, _P_W0, _P_A0, _P_KK, _P_KA, _P_RK, _P_LNW, _P_LNB = range(11)
_P_ROWS = 16


def _rwkv_kernel(pr_ref, pk_ref, pv_ref, pl_ref, pg_ref, par_ref, w2_ref, a2_ref, s0_ref, prev_ref,
                 y_ref, sout_ref,
                 state, carry, r_s, k_s, v_s, kk_s, a_s, lw_s, y_s, *, tile, ng):
    t_idx = pl.program_id(2)

    @pl.when(t_idx == 0)
    def _():
        state[...] = s0_ref[...]
        carry[...] = prev_ref[0]

    par = par_ref[0]
    prow = lambda i: par[i:i + 1, :]
    ones_bd = (lax.broadcasted_iota(jnp.int32, (GL, GL), 0) // HEAD
               == lax.broadcasted_iota(jnp.int32, (GL, GL), 1) // HEAD).astype(BF16)
    first_row = lax.broadcasted_iota(jnp.int32, (tile, ng * GL), 0) == 0

    def head_sum(xb):
        return jnp.concatenate([jnp.dot(xb[:, g * GL:(g + 1) * GL], ones_bd, preferred_element_type=F32)
                                for g in range(ng)], axis=1)

    def shift_lerp(p_ref, idx, mu, first):
        p = p_ref[...]
        prev = jnp.where(first, carry[idx:idx + 1, :p.shape[1]], pltpu.roll(p, 1, 0))
        carry[idx:idx + 1, :p.shape[1]] = p[tile - 1:tile, :]
        return p + (prev - p) * mu

    r = shift_lerp(pr_ref, 0, prow(_P_MU_R), first_row)
    k = shift_lerp(pk_ref, 1, prow(_P_MU_K), first_row)
    v = shift_lerp(pv_ref, 2, prow(_P_MU_V), first_row)
    first_row_l = lax.broadcasted_iota(jnp.int32, (tile, 2 * LORA), 0) == 0
    zl = shift_lerp(pl_ref, 3, par[_P_MU_L:_P_MU_L + 1, :2 * LORA], first_row_l)
    zw = jnp.tanh(zl[:, :LORA]).astype(BF16)
    za = zl[:, LORA:].astype(BF16)
    xw = prow(_P_W0) + jnp.dot(zw, w2_ref[...], preferred_element_type=F32)
    lw = LOG2_DECAY_SCALE * _sigmoid(xw)
    a = _sigmoid(prow(_P_A0) + jnp.dot(za, a2_ref[...], preferred_element_type=F32))
    kk = k * prow(_P_KK)
    kk = kk * lax.rsqrt(jnp.maximum(head_sum((kk * kk).astype(BF16)), 1e-24))
    k = k * (1.0 + (a - 1.0) * prow(_P_KA))
    r_s[...] = r
    k_s[...] = k
    v_s[...] = v
    kk_s[...] = kk
    a_s[...] = a
    lw_s[...] = lw

    ti = lax.broadcasted_iota(jnp.int32, (CHUNK, GL), 0)
    lane = lax.broadcasted_iota(jnp.int32, (CHUNK, GL), 1)
    si = lane % HEAD
    strict = si < ti
    incl = si <= ti
    eye = jnp.where(si == ti, 1.0, 0.0).astype(F32)
    head_of_lane = [lane // HEAD == h for h in range(GROUP)]
    bd_mask = (lax.broadcasted_iota(jnp.int32, (GL, GL), 0) // HEAD
               == lax.broadcasted_iota(jnp.int32, (GL, GL), 1) // HEAD)
    tri2 = (lax.broadcasted_iota(jnp.int32, (CHUNK, 2 * CHUNK), 1) % CHUNK
            <= lax.broadcasted_iota(jnp.int32, (CHUNK, 2 * CHUNK), 0)).astype(BF16)

    def bd(x):
        xb = x.astype(BF16)
        zero = jnp.zeros((), BF16)
        return jnp.concatenate([jnp.where(m, xb, zero) for m in head_of_lane], axis=0)

    def rob_mm(x, y):
        return jnp.dot(x.astype(BF16), bd(y), preferred_element_type=F32)

    def nt_dot(x, y):
        return lax.dot_general(x, y, (((1,), (1,)), ((), ())), preferred_element_type=F32)

    def group_chunk(rows, g):
        cols = slice(g * GL, (g + 1) * GL)
        rc, kc, vc, kkc, ac, lwc = (s[rows, cols] for s in (r_s, k_s, v_s, kk_s, a_s, lw_s))
        hi = lwc.astype(BF16)
        lo = (lwc - hi.astype(F32)).astype(BF16)
        cum = jnp.dot(tri2, jnp.concatenate([hi, lo], axis=0), preferred_element_type=F32)
        yield
        cum_end = cum[CHUNK - 1:CHUNK, :]
        e_cum = jnp.exp2(cum)
        e_neg = jnp.exp2(-cum)
        e_prev = jnp.exp2(cum - lwc)
        e_end = jnp.exp2(cum_end - cum)
        kka = kkc * ac
        al = -kkc * e_prev
        rt = rc * e_cum
        bt = kka * e_neg
        kt = kc * e_neg
        bh = kka * e_end
        kh = kc * e_end
        lr = jnp.concatenate([al, rt], axis=0).astype(BF16)
        x_b = nt_dot(lr, bd(bt))
        x_k = nt_dot(lr, bd(kt))
        s_old = state[g]
        ars = nt_dot(lr, s_old.astype(BF16))
        yield
        a_ab = jnp.where(strict, x_b[:CHUNK], 0.0)
        a_rb = jnp.where(incl, x_b[CHUNK:], 0.0)
        a_ak = jnp.where(strict, x_k[:CHUNK], 0.0)
        a_rk = jnp.where(incl, x_k[CHUNK:], 0.0)
        av = ars + rob_mm(jnp.concatenate([a_ak, a_rk], axis=0), vc)
        steps = int(math.log2(CHUNK))
        pw = rob_mm(a_ab, a_ab)
        tinv = eye + a_ab
        yield
        for _ in range(steps - 2):
            both = rob_mm(jnp.concatenate([pw, tinv], axis=0), pw)
            pw = both[:CHUNK]
            tinv = tinv + both[CHUNK:]
            yield
        tinv = tinv + rob_mm(tinv, pw)
        yield
        u = rob_mm(tinv, av[:CHUNK])
        yield
        y_s[rows, cols] = av[CHUNK:] + rob_mm(a_rb, u)
        uv = jnp.concatenate([u, vc], axis=0).astype(BF16)
        bk = jnp.concatenate([bh, kh], axis=0).astype(BF16)
        ds = lax.dot_general(uv, bk, (((0,), (0,)), ((), ())), preferred_element_type=F32)
        state[g] = s_old * jnp.exp2(cum_end) + jnp.where(bd_mask, ds, 0.0)

    def chunk_body(c, _):
        rows = pl.ds(pl.multiple_of(c * CHUNK, CHUNK), CHUNK)
        live = [group_chunk(rows, g) for g in range(ng)]
        while live:
            live = [gen for gen in live if next(gen, StopIteration) is not StopIteration]
        return 0

    lax.fori_loop(0, tile // CHUNK, chunk_body, 0)

    y = y_s[...]
    inv_n = 1.0 / HEAD
    y_hi = y.astype(BF16)
    y_lo = (y - y_hi.astype(F32)).astype(BF16)
    d = y - (head_sum(y_hi) + head_sum(y_lo)) * inv_n
    var = head_sum((d * d).astype(BF16)) * inv_n
    yn = d * lax.rsqrt(var + GN_EPS) * prow(_P_LNW) + prow(_P_LNB)
    bonus = head_sum((r_s[...] * k_s[...] * prow(_P_RK)).astype(BF16)) * v_s[...]
    y_ref[...] = (_sigmoid(pg_ref[...]) * (yn + bonus)).astype(y_ref.dtype)

    @pl.when(t_idx == pl.num_programs(2) - 1)
    def _():
        sout_ref[0] = state[...]


def _rwkv(p_rkv, p_lora, p_gate, par, w2, a2, s0, prev, batch, tile, ng):
    rows = p_rkv.shape[0]
    d = p_rkv.shape[1] // 3
    groups = d // GL
    sg = groups // ng
    width = ng * GL
    nt = rows // batch // tile
    wide = lambda arr: arr.reshape(sg, ng, arr.shape[1], GL).transpose(0, 2, 1, 3).reshape(sg, arr.shape[1], width)
    row_blk = lambda off: pl.BlockSpec((tile, width), lambda b, g, t: (b * nt + t, off + g))
    scratch = ([pltpu.VMEM((ng, GL, GL), F32), pltpu.VMEM((8, width), F32)]
               + [pltpu.VMEM((tile, width), F32)] * 7)
    return pl.pallas_call(
        functools.partial(_rwkv_kernel, tile=tile, ng=ng),
        grid=(batch, sg, nt),
        in_specs=[row_blk(0), row_blk(sg), row_blk(2 * sg),
                  pl.BlockSpec((tile, 2 * LORA), lambda b, g, t: (b * nt + t, 0)),
                  row_blk(0),
                  pl.BlockSpec((1, _P_ROWS, width), lambda b, g, t: (g, 0, 0)),
                  pl.BlockSpec((LORA, width), lambda b, g, t: (0, g)),
                  pl.BlockSpec((LORA, width), lambda b, g, t: (0, g)),
                  pl.BlockSpec((ng, GL, GL), lambda b, g, t: (g, 0, 0)),
                  pl.BlockSpec((1, 8, width), lambda b, g, t: (g, 0, 0))],
        out_specs=[row_blk(0), pl.BlockSpec((1, ng, GL, GL), lambda b, g, t: (b * sg + g, 0, 0, 0))],
        out_shape=[jax.ShapeDtypeStruct((rows, d), BF16), jax.ShapeDtypeStruct((batch * sg, ng, GL, GL), F32)],
        scratch_shapes=scratch,
        compiler_params=_params("parallel", "parallel", "arbitrary"),
        name="rwkv7",
    )(p_rkv, p_rkv, p_rkv, p_lora, p_gate, wide(par), w2, a2, s0, wide(prev))


def _conv_proj_kernel(u_ref, wb_ref, wc_ref, wh_ref, wg_ref, cw_ref, mc_ref, mh_ref, o_ref, carry,
                      *, tile, tiles_per_seq):
    i = pl.program_id(1)

    @pl.when(i % tiles_per_seq == 0)
    def _():
        carry[...] = mc_ref[...] * mh_ref[...]

    x = u_ref[...]
    proj = lambda w_ref: jnp.dot(x, w_ref[...], preferred_element_type=F32)
    u = proj(wc_ref) * proj(wh_ref)
    row = lax.broadcasted_iota(jnp.int32, u.shape, 0)
    prev1 = jnp.where(row == 0, carry[7:8, :], pltpu.roll(u, 1, 0))
    prev2 = jnp.where(row == 0, carry[6:7, :], jnp.where(row == 1, carry[7:8, :], pltpu.roll(u, 2, 0)))
    carry[...] = u[tile - 8:tile, :]
    w = cw_ref[...]
    conv = w[0:1, :] * prev2 + w[1:2, :] * prev1 + w[2:3, :] * u
    o_ref[...] = (_sigmoid(proj(wg_ref)) * (proj(wb_ref) * conv)).astype(o_ref.dtype)


def _conv_proj(u, w_conv, w_gate, conv_w, p_conv_meta, batch, tile, tc):
    rows, k = u.shape
    d = w_gate.shape[1]
    nc = d // tc
    tiles_per_seq = rows // batch // tile
    meta_blk = p_conv_meta.shape[0] // 8 - 1
    wblk = lambda off: pl.BlockSpec((k, tc), lambda j, i: (0, off + j))
    w8 = jnp.zeros((8, d), F32).at[:conv_w.shape[0]].set(conv_w)
    return pl.pallas_call(
        functools.partial(_conv_proj_kernel, tile=tile, tiles_per_seq=tiles_per_seq),
        grid=(nc, rows // tile),
        in_specs=[pl.BlockSpec((tile, k), lambda j, i: (i, 0)),
                  wblk(0), wblk(nc), wblk(2 * nc), wblk(0),
                  pl.BlockSpec((8, tc), lambda j, i: (0, j)),
                  pl.BlockSpec((8, tc), lambda j, i: (meta_blk, nc + j)),
                  pl.BlockSpec((8, tc), lambda j, i: (meta_blk, 2 * nc + j))],
        out_specs=pl.BlockSpec((tile, tc), lambda j, i: (i, j)),
        out_shape=jax.ShapeDtypeStruct((rows, d), BF16),
        scratch_shapes=[pltpu.VMEM((8, tc), F32)],
        compiler_params=_params("parallel", "arbitrary"),
        name="conv_proj",
    )(u, w_conv, w_conv, w_conv, w_gate, w8, p_conv_meta, p_conv_meta)


def _pick(n, prefs):
    for p in prefs:
        if n % p == 0:
            return p
    raise ValueError(f"no tile for {n} in {prefs}")


def kernel(x, meta_tokens, norm_mix_g, w_in, rwkv_shift_mu, rwkv_w0, rwkv_w2, rwkv_a0, rwkv_a2, rwkv_k_k, rwkv_k_a, rwkv_r_k, rwkv_ln_w, rwkv_ln_b, conv_w, w_out, norm_mlp_g, w_up, w_down, norm_final_g):
    bsz, seq, d = x.shape
    n_meta = meta_tokens.shape[0]
    depth = w_in.shape[0]
    assert depth == 1, "the meta-token hand-off is written for a single layer"
    assert d % GL == 0 and seq % CHUNK == 0 and n_meta <= META_ROWS
    groups = d // GL
    rows = bsz * seq
    layer = 0

    c_rkv, c_lora = 3 * d, 2 * LORA
    c0 = c_rkv + c_lora
    w_all = w_in[layer]
    w_rkv = w_all[:, :c_rkv].astype(BF16)
    w_lora = w_all[:, c_rkv:c0].astype(BF16)
    w_conv = w_all[:, c0:c0 + 3 * d].astype(BF16)
    w_ga = w_all[:, c0 + 3 * d:c0 + 4 * d].astype(BF16)
    w_gb = w_all[:, c0 + 4 * d:].astype(BF16)
    wo_a = w_out[layer, :d].astype(BF16)
    wo_b = w_out[layer, d:].astype(BF16)
    wu = w_up[layer].astype(BF16)
    wd = w_down[layer].astype(BF16)
    w2 = rwkv_w2[layer].astype(BF16)
    a2 = rwkv_a2[layer].astype(BF16)

    mu = rwkv_shift_mu[layer]
    per_group = lambda vec: vec.reshape(groups, 1, GL)
    par = jnp.concatenate(
        [per_group(mu[:d]), per_group(mu[d:2 * d]), per_group(mu[2 * d:3 * d]),
         jnp.broadcast_to(mu[c_rkv:c0].reshape(1, 1, GL), (groups, 1, GL)),
         per_group(rwkv_w0[layer]), per_group(rwkv_a0[layer]), per_group(rwkv_k_k[layer]),
         per_group(rwkv_k_a[layer]), per_group(rwkv_r_k[layer].reshape(d)),
         per_group(rwkv_ln_w[layer]), per_group(rwkv_ln_b[layer]),
         jnp.zeros((groups, _P_ROWS - 11, GL), F32)], axis=1)

    def in_proj(h_rows, tm, weights):
        u = _rmsnorm(h_rows, norm_mix_g[layer], BF16, min(tm, 256))
        return u, [_matmul(u, w, tm, _pick(w.shape[1], (1024, 512, 256))) for w in weights]

    meta = jnp.zeros((META_ROWS, d), F32).at[META_ROWS - n_meta:].set(meta_tokens.astype(F32))
    _, (m_rkv, m_lora, m_gate, m_conv) = in_proj(meta, META_ROWS, (w_rkv, w_lora, w_ga, w_conv))
    ng = _pick(groups, (8, 4, 2, 1))
    _, s_meta = _rwkv(m_rkv, m_lora, m_gate, par, w2, a2, jnp.zeros((groups, GL, GL), F32),
                      jnp.zeros((groups, 8, GL), F32), 1, META_ROWS, ng)
    last = jnp.concatenate([m_rkv[-1].reshape(3, groups, GL).transpose(1, 0, 2),
                            jnp.broadcast_to(m_lora[-1].reshape(1, 1, GL), (groups, 1, GL)),
                            jnp.zeros((groups, 4, GL), F32)], axis=1)

    xf = x.reshape(rows, d)
    tm = _pick(rows, (1024, 512, 256, 128, 64))
    u, (p_rkv, p_lora, p_gate) = in_proj(xf, tm, (w_rkv, w_lora, w_ga))
    t_scan = _pick(seq, (256, 128, 64))
    ya, _ = _rwkv(p_rkv, p_lora, p_gate, par, w2, a2, s_meta.reshape(groups, GL, GL), last, bsz, t_scan, ng)
    yb = _conv_proj(u, w_conv, w_gb, conv_w[layer], m_conv, bsz, _pick(seq, (1024, 512, 256, 128, 64)),
                    _pick(d, (256, 128)))
    tq = _pick(rows, (512, 256, 128, 64))
    h1 = _out_proj(ya, yb, wo_a, wo_b, xf, tq, _pick(d, (512, 256, 128)))
    u2 = _rmsnorm(h1, norm_mlp_g[layer], BF16, min(tq, 256))
    big = (1024, 512, 256, 128)
    hid = _mlp_up(u2, wu, tm, _pick(wu.shape[1], big))
    h2 = _mlp_down(hid, wd, h1, tm, _pick(d, big), _pick(wu.shape[1], (2048, 1024, 512)))
    out = _rmsnorm(h2, norm_final_g, F32, min(tq, 256))
    return out.reshape(bsz, seq, d)
```

```python
import functools
import math

import jax
import jax.numpy as jnp
from jax import lax
from jax.experimental import pallas as pl
from jax.experimental.pallas import tpu as pltpu

F32 = jnp.float32
BF16 = jnp.bfloat16

HEAD = 64
GROUP = 4
GL = GROUP * HEAD
CHUNK = 64
LORA = 128
NORM_EPS = 1e-6
GN_EPS = 64e-5
LOG2_DECAY_SCALE = -math.exp(-0.5) / math.log(2.0)
META_ROWS = 64
VMEM_LIMIT = 56 * 1024 * 1024


def _params(*sem):
    return pltpu.CompilerParams(dimension_semantics=sem, vmem_limit_bytes=VMEM_LIMIT)


def _sigmoid(x):
    return 0.5 * jnp.tanh(0.5 * x) + 0.5


def _rmsnorm_kernel(x_ref, g_ref, o_ref):
    x = x_ref[...]
    y = x * lax.rsqrt(jnp.mean(x * x, axis=-1, keepdims=True) + NORM_EPS)
    o_ref[...] = (y * g_ref[...]).astype(o_ref.dtype)


def _rmsnorm(x, g, out_dtype, tm):
    rows, d = x.shape
    return pl.pallas_call(
        _rmsnorm_kernel,
        grid=(rows // tm,),
        in_specs=[pl.BlockSpec((tm, d), lambda i: (i, 0)), pl.BlockSpec((1, d), lambda i: (0, 0))],
        out_specs=pl.BlockSpec((tm, d), lambda i: (i, 0)),
        out_shape=jax.ShapeDtypeStruct((rows, d), out_dtype),
        compiler_params=_params("parallel"),
        name="rmsnorm",
    )(x, g.reshape(1, d))


def _matmul_kernel(x_ref, w_ref, o_ref, *, gate):
    p = jnp.dot(x_ref[...], w_ref[...], preferred_element_type=F32)
    o_ref[...] = _sigmoid(p) if gate else p


def _matmul(x, w, tm, tn, gate=False):
    rows, k = x.shape
    n = w.shape[1]
    return pl.pallas_call(
        functools.partial(_matmul_kernel, gate=gate),
        grid=(rows // tm, n // tn),
        in_specs=[pl.BlockSpec((tm, k), lambda i, j: (i, 0)), pl.BlockSpec((k, tn), lambda i, j: (0, j))],
        out_specs=pl.BlockSpec((tm, tn), lambda i, j: (i, j)),
        out_shape=jax.ShapeDtypeStruct((rows, n), F32),
        compiler_params=_params("parallel", "parallel"),
        name="in_proj",
    )(x, w)


def _shift_proj_kernel(x_ref, w_ref, mu_ref, m_ref, o_ref, carry, *, tile, tiles_per_seq):
    i = pl.program_id(1)

    @pl.when(i % tiles_per_seq == 0)
    def _():
        carry[...] = m_ref[...]

    p = jnp.dot(x_ref[...], w_ref[...], preferred_element_type=F32)
    row = lax.broadcasted_iota(jnp.int32, p.shape, 0)
    prev = jnp.where(row == 0, carry[7:8, :], pltpu.roll(p, 1, 0))
    carry[...] = p[tile - 8:tile, :]
    o_ref[...] = p + (prev - p) * mu_ref[0:1, :]


def _shift_proj(x, w, mu, p_before, batch, tm, tn):
    rows, k = x.shape
    n = w.shape[1]
    tiles_per_seq = rows // batch // tm
    m_blk = p_before.shape[0] // 8 - 1
    mu8 = jnp.zeros((8, n), F32).at[0].set(mu)
    return pl.pallas_call(
        functools.partial(_shift_proj_kernel, tile=tm, tiles_per_seq=tiles_per_seq),
        grid=(n // tn, rows // tm),
        in_specs=[pl.BlockSpec((tm, k), lambda j, i: (i, 0)), pl.BlockSpec((k, tn), lambda j, i: (0, j)),
                  pl.BlockSpec((8, tn), lambda j, i: (0, j)), pl.BlockSpec((8, tn), lambda j, i: (m_blk, j))],
        out_specs=pl.BlockSpec((tm, tn), lambda j, i: (i, j)),
        out_shape=jax.ShapeDtypeStruct((rows, n), F32),
        scratch_shapes=[pltpu.VMEM((8, tn), F32)],
        compiler_params=_params("parallel", "arbitrary"),
        name="shift_proj",
    )(x, w, mu8, p_before)


def _out_proj_kernel(a_ref, b_ref, wa_ref, wb_ref, h_ref, o_ref):
    acc = jnp.dot(a_ref[...], wa_ref[...], preferred_element_type=F32)
    acc += jnp.dot(b_ref[...], wb_ref[...], preferred_element_type=F32)
    o_ref[...] = h_ref[...] + acc


def _out_proj(ya, yb, wa, wb, h, tm, tn):
    rows, k = ya.shape
    n = wa.shape[1]
    return pl.pallas_call(
        _out_proj_kernel,
        grid=(rows // tm, n // tn),
        in_specs=[pl.BlockSpec((tm, k), lambda i, j: (i, 0)), pl.BlockSpec((tm, k), lambda i, j: (i, 0)),
                  pl.BlockSpec((k, tn), lambda i, j: (0, j)), pl.BlockSpec((k, tn), lambda i, j: (0, j)),
                  pl.BlockSpec((tm, tn), lambda i, j: (i, j))],
        out_specs=pl.BlockSpec((tm, tn), lambda i, j: (i, j)),
        out_shape=jax.ShapeDtypeStruct((rows, n), F32),
        compiler_params=_params("parallel", "parallel"),
        name="out_proj",
    )(ya, yb, wa, wb, h)


def _mlp_up_kernel(x_ref, w_ref, o_ref):
    hid = jnp.dot(x_ref[...], w_ref[...], preferred_element_type=F32)
    o_ref[...] = jnp.square(jnp.maximum(hid, 0.0)).astype(o_ref.dtype)


def _mlp_up(x, w, tm, tn):
    rows, k = x.shape
    n = w.shape[1]
    return pl.pallas_call(
        _mlp_up_kernel,
        grid=(rows // tm, n // tn),
        in_specs=[pl.BlockSpec((tm, k), lambda i, j: (i, 0)), pl.BlockSpec((k, tn), lambda i, j: (0, j))],
        out_specs=pl.BlockSpec((tm, tn), lambda i, j: (i, j)),
        out_shape=jax.ShapeDtypeStruct((rows, n), BF16),
        compiler_params=_params("parallel", "parallel"),
        name="mlp_up",
    )(x, w)


def _mlp_down_kernel(x_ref, w_ref, h_ref, o_ref):
    part = jnp.dot(x_ref[...], w_ref[...], preferred_element_type=F32)

    @pl.when(pl.program_id(2) == 0)
    def _():
        o_ref[...] = h_ref[...] + part

    @pl.when(pl.program_id(2) > 0)
    def _():
        o_ref[...] += part


def _mlp_down(x, w, h, tm, tn, tk):
    rows, k = x.shape
    n = w.shape[1]
    return pl.pallas_call(
        _mlp_down_kernel,
        grid=(rows // tm, n // tn, k // tk),
        in_specs=[pl.BlockSpec((tm, tk), lambda i, j, q: (i, q)), pl.BlockSpec((tk, tn), lambda i, j, q: (q, j)),
                  pl.BlockSpec((tm, tn), lambda i, j, q: (i, j))],
        out_specs=pl.BlockSpec((tm, tn), lambda i, j, q: (i, j)),
        out_shape=jax.ShapeDtypeStruct((rows, n), F32),
        compiler_params=_params("parallel", "parallel", "arbitrary"),
        name="mlp_down",
    )(x, w, h)


_P_W0, _P_A0, _P_KK, _P_KA, _P_RK, _P_LNW, _P_LNB = range(7)
_P_ROWS = 8


def _rwkv_kernel(zr_ref, zk_ref, zv_ref, zl_ref, gate_ref, par_ref, w2_ref, a2_ref, s0_ref,
                 y_ref, sout_ref,
                 state, r_s, k_s, v_s, kk_s, a_s, lw_s, y_s, *, tile, ng):
    t_idx = pl.program_id(2)

    @pl.when(t_idx == 0)
    def _():
        state[...] = s0_ref[...]

    par = par_ref[0]
    prow = lambda i: par[i:i + 1, :]
    ones_bd = (lax.broadcasted_iota(jnp.int32, (GL, GL), 0) // HEAD
               == lax.broadcasted_iota(jnp.int32, (GL, GL), 1) // HEAD).astype(BF16)

    def head_sum(xb):
        return jnp.concatenate([jnp.dot(xb[:, g * GL:(g + 1) * GL], ones_bd, preferred_element_type=F32)
                                for g in range(ng)], axis=1)

    k = zk_ref[...]
    zw = jnp.tanh(zl_ref[:, :LORA]).astype(BF16)
    za = zl_ref[:, LORA:].astype(BF16)
    xw = prow(_P_W0) + jnp.dot(zw, w2_ref[...], preferred_element_type=F32)
    lw = LOG2_DECAY_SCALE * _sigmoid(xw)
    a = _sigmoid(prow(_P_A0) + jnp.dot(za, a2_ref[...], preferred_element_type=F32))
    kk = k * prow(_P_KK)
    kk = kk * lax.rsqrt(jnp.maximum(head_sum((kk * kk).astype(BF16)), 1e-24))
    k = k * (1.0 + (a - 1.0) * prow(_P_KA))
    r_s[...] = zr_ref[...]
    k_s[...] = k
    v_s[...] = zv_ref[...]
    kk_s[...] = kk
    a_s[...] = a
    lw_s[...] = lw

    ti = lax.broadcasted_iota(jnp.int32, (CHUNK, GL), 0)
    lane = lax.broadcasted_iota(jnp.int32, (CHUNK, GL), 1)
    si = lane % HEAD
    strict = si < ti
    incl = si <= ti
    eye = jnp.where(si == ti, 1.0, 0.0).astype(F32)
    head_of_lane = [lane // HEAD == h for h in range(GROUP)]
    bd_mask = (lax.broadcasted_iota(jnp.int32, (GL, GL), 0) // HEAD
               == lax.broadcasted_iota(jnp.int32, (GL, GL), 1) // HEAD)
    tri2 = (lax.broadcasted_iota(jnp.int32, (CHUNK, 2 * CHUNK), 1) % CHUNK
            <= lax.broadcasted_iota(jnp.int32, (CHUNK, 2 * CHUNK), 0)).astype(BF16)

    def bd(x):
        xb = x.astype(BF16)
        zero = jnp.zeros((), BF16)
        return jnp.concatenate([jnp.where(m, xb, zero) for m in head_of_lane], axis=0)

    def rob_mm(x, y):
        return jnp.dot(x.astype(BF16), bd(y), preferred_element_type=F32)

    def nt_dot(x, y):
        return lax.dot_general(x, y, (((1,), (1,)), ((), ())), preferred_element_type=F32)

    def group_chunk(rows, g):
        cols = slice(g * GL, (g + 1) * GL)
        rc, kc, vc, kkc, ac, lwc = (s[rows, cols] for s in (r_s, k_s, v_s, kk_s, a_s, lw_s))
        hi = lwc.astype(BF16)
        lo = (lwc - hi.astype(F32)).astype(BF16)
        cum = jnp.dot(tri2, jnp.concatenate([hi, lo], axis=0), preferred_element_type=F32)
        yield
        cum_end = cum[CHUNK - 1:CHUNK, :]
        e_cum = jnp.exp2(cum)
        e_neg = jnp.exp2(-cum)
        e_prev = jnp.exp2(cum - lwc)
        e_end = jnp.exp2(cum_end - cum)
        kka = kkc * ac
        al = -kkc * e_prev
        rt = rc * e_cum
        bt = kka * e_neg
        kt = kc * e_neg
        bh = kka * e_end
        kh = kc * e_end
        lr = jnp.concatenate([al, rt], axis=0).astype(BF16)
        x_b = nt_dot(lr, bd(bt))
        x_k = nt_dot(lr, bd(kt))
        s_old = state[g]
        ars = nt_dot(lr, s_old.astype(BF16))
        yield
        a_ab = jnp.where(strict, x_b[:CHUNK], 0.0)
        a_rb = jnp.where(incl, x_b[CHUNK:], 0.0)
        a_ak = jnp.where(strict, x_k[:CHUNK], 0.0)
        a_rk = jnp.where(incl, x_k[CHUNK:], 0.0)
        av = ars + rob_mm(jnp.concatenate([a_ak, a_rk], axis=0), vc)
        steps = int(math.log2(CHUNK))
        pw = rob_mm(a_ab, a_ab)
        tinv = eye + a_ab
        yield
        for _ in range(steps - 2):
            both = rob_mm(jnp.concatenate([pw, tinv], axis=0), pw)
            pw = both[:CHUNK]
            tinv = tinv + both[CHUNK:]
            yield
        tinv = tinv + rob_mm(tinv, pw)
        yield
        u = rob_mm(tinv, av[:CHUNK])
        yield
        y_s[rows, cols] = av[CHUNK:] + rob_mm(a_rb, u)
        uv = jnp.concatenate([u, vc], axis=0).astype(BF16)
        bk = jnp.concatenate([bh, kh], axis=0).astype(BF16)
        ds = lax.dot_general(uv, bk, (((0,), (0,)), ((), ())), preferred_element_type=F32)
        state[g] = s_old * jnp.exp2(cum_end) + jnp.where(bd_mask, ds, 0.0)

    def chunk_body(c, _):
        rows = pl.ds(pl.multiple_of(c * CHUNK, CHUNK), CHUNK)
        live = [group_chunk(rows, g) for g in range(ng)]
        while live:
            live = [gen for gen in live if next(gen, StopIteration) is not StopIteration]
        return 0

    lax.fori_loop(0, tile // CHUNK, chunk_body, 0)

    y = y_s[...]
    inv_n = 1.0 / HEAD
    y_hi = y.astype(BF16)
    y_lo = (y - y_hi.astype(F32)).astype(BF16)
    d = y - (head_sum(y_hi) + head_sum(y_lo)) * inv_n
    var = head_sum((d * d).astype(BF16)) * inv_n
    yn = d * lax.rsqrt(var + GN_EPS) * prow(_P_LNW) + prow(_P_LNB)
    bonus = head_sum((r_s[...] * k_s[...] * prow(_P_RK)).astype(BF16)) * v_s[...]
    y_ref[...] = (gate_ref[...] * (yn + bonus)).astype(y_ref.dtype)

    @pl.when(t_idx == pl.num_programs(2) - 1)
    def _():
        sout_ref[0] = state[...]


def _rwkv(z_rkv, z_lora, gate, par, w2, a2, s0, batch, tile, ng):
    rows = z_rkv.shape[0]
    d = z_rkv.shape[1] // 3
    groups = d // GL
    sg = groups // ng
    width = ng * GL
    nt = rows // batch // tile
    wide = lambda arr: arr.reshape(sg, ng, arr.shape[1], GL).transpose(0, 2, 1, 3).reshape(sg, arr.shape[1], width)
    row_blk = lambda off: pl.BlockSpec((tile, width), lambda b, g, t: (b * nt + t, off + g))
    scratch = [pltpu.VMEM((ng, GL, GL), F32)] + [pltpu.VMEM((tile, width), F32)] * 7
    return pl.pallas_call(
        functools.partial(_rwkv_kernel, tile=tile, ng=ng),
        grid=(batch, sg, nt),
        in_specs=[row_blk(0), row_blk(sg), row_blk(2 * sg),
                  pl.BlockSpec((tile, 2 * LORA), lambda b, g, t: (b * nt + t, 0)),
                  row_blk(0),
                  pl.BlockSpec((1, _P_ROWS, width), lambda b, g, t: (g, 0, 0)),
                  pl.BlockSpec((LORA, width), lambda b, g, t: (0, g)),
                  pl.BlockSpec((LORA, width), lambda b, g, t: (0, g)),
                  pl.BlockSpec((ng, GL, GL), lambda b, g, t: (g, 0, 0))],
        out_specs=[row_blk(0), pl.BlockSpec((1, ng, GL, GL), lambda b, g, t: (b * sg + g, 0, 0, 0))],
        out_shape=[jax.ShapeDtypeStruct((rows, d), BF16), jax.ShapeDtypeStruct((batch * sg, ng, GL, GL), F32)],
        scratch_shapes=scratch,
        compiler_params=_params("parallel", "parallel", "arbitrary"),
        name="rwkv7",
    )(z_rkv, z_rkv, z_rkv, z_lora, gate, wide(par), w2, a2, s0)


def _conv_proj_kernel(u_ref, wb_ref, wc_ref, wh_ref, wg_ref, cw_ref, mc_ref, mh_ref, o_ref, carry,
                      *, tile, tiles_per_seq):
    i = pl.program_id(1)

    @pl.when(i % tiles_per_seq == 0)
    def _():
        carry[...] = mc_ref[...] * mh_ref[...]

    x = u_ref[...]
    proj = lambda w_ref: jnp.dot(x, w_ref[...], preferred_element_type=F32)
    u = proj(wc_ref) * proj(wh_ref)
    row = lax.broadcasted_iota(jnp.int32, u.shape, 0)
    prev1 = jnp.where(row == 0, carry[7:8, :], pltpu.roll(u, 1, 0))
    prev2 = jnp.where(row == 0, carry[6:7, :], jnp.where(row == 1, carry[7:8, :], pltpu.roll(u, 2, 0)))
    carry[...] = u[tile - 8:tile, :]
    w = cw_ref[...]
    conv = w[0:1, :] * prev2 + w[1:2, :] * prev1 + w[2:3, :] * u
    o_ref[...] = (_sigmoid(proj(wg_ref)) * (proj(wb_ref) * conv)).astype(o_ref.dtype)


def _conv_proj(u, w_conv, w_gate, conv_w, p_conv_meta, batch, tile, tc):
    rows, k = u.shape
    d = w_gate.shape[1]
    nc = d // tc
    tiles_per_seq = rows // batch // tile
    meta_blk = p_conv_meta.shape[0] // 8 - 1
    wblk = lambda off: pl.BlockSpec((k, tc), lambda j, i: (0, off + j))
    w8 = jnp.zeros((8, d), F32).at[:conv_w.shape[0]].set(conv_w)
    return pl.pallas_call(
        functools.partial(_conv_proj_kernel, tile=tile, tiles_per_seq=tiles_per_seq),
        grid=(nc, rows // tile),
        in_specs=[pl.BlockSpec((tile, k), lambda j, i: (i, 0)),
                  wblk(0), wblk(nc), wblk(2 * nc), wblk(0),
                  pl.BlockSpec((8, tc), lambda j, i: (0, j)),
                  pl.BlockSpec((8, tc), lambda j, i: (meta_blk, nc + j)),
                  pl.BlockSpec((8, tc), lambda j, i: (meta_blk, 2 * nc + j))],
        out_specs=pl.BlockSpec((tile, tc), lambda j, i: (i, j)),
        out_shape=jax.ShapeDtypeStruct((rows, d), BF16),
        scratch_shapes=[pltpu.VMEM((8, tc), F32)],
        compiler_params=_params("parallel", "arbitrary"),
        name="conv_proj",
    )(u, w_conv, w_conv, w_conv, w_gate, w8, p_conv_meta, p_conv_meta)


def _pick(n, prefs):
    for p in prefs:
        if n % p == 0:
            return p
    raise ValueError(f"no tile for {n} in {prefs}")


def kernel(x, meta_tokens, norm_mix_g, w_in, rwkv_shift_mu, rwkv_w0, rwkv_w2, rwkv_a0, rwkv_a2, rwkv_k_k, rwkv_k_a, rwkv_r_k, rwkv_ln_w, rwkv_ln_b, conv_w, w_out, norm_mlp_g, w_up, w_down, norm_final_g):
    bsz, seq, d = x.shape
    n_meta = meta_tokens.shape[0]
    depth = w_in.shape[0]
    assert depth == 1, "the meta-token hand-off is written for a single layer"
    assert d % GL == 0 and seq % CHUNK == 0 and n_meta <= META_ROWS
    groups = d // GL
    rows = bsz * seq
    layer = 0

    c_rkv, c_lora = 3 * d, 2 * LORA
    c0 = c_rkv + c_lora
    w_all = w_in[layer]
    w_rkv = w_all[:, :c_rkv].astype(BF16)
    w_lora = w_all[:, c_rkv:c0].astype(BF16)
    w_conv = w_all[:, c0:c0 + 3 * d].astype(BF16)
    w_ga = w_all[:, c0 + 3 * d:c0 + 4 * d].astype(BF16)
    w_gb = w_all[:, c0 + 4 * d:].astype(BF16)
    wo_a = w_out[layer, :d].astype(BF16)
    wo_b = w_out[layer, d:].astype(BF16)
    wu = w_up[layer].astype(BF16)
    wd = w_down[layer].astype(BF16)
    w2 = rwkv_w2[layer].astype(BF16)
    a2 = rwkv_a2[layer].astype(BF16)

    mu = rwkv_shift_mu[layer]
    per_group = lambda vec: vec.reshape(groups, 1, GL)
    par = jnp.concatenate(
        [per_group(rwkv_w0[layer]), per_group(rwkv_a0[layer]), per_group(rwkv_k_k[layer]),
         per_group(rwkv_k_a[layer]), per_group(rwkv_r_k[layer].reshape(d)),
         per_group(rwkv_ln_w[layer]), per_group(rwkv_ln_b[layer]),
         jnp.zeros((groups, _P_ROWS - 7, GL), F32)], axis=1)
    wide_tiles = (1024, 512, 256)

    def mixer_inputs(h_rows, batch, tm, before):
        u = _rmsnorm(h_rows, norm_mix_g[layer], BF16, min(tm, 256))
        z_rkv = _shift_proj(u, w_rkv, mu[:c_rkv], before[0], batch, tm, _pick(c_rkv, wide_tiles))
        z_lora = _shift_proj(u, w_lora, mu[c_rkv:c0], before[1], batch, tm, c_lora)
        return u, z_rkv, z_lora, _matmul(u, w_ga, tm, _pick(d, wide_tiles), gate=True)

    meta = jnp.zeros((META_ROWS, d), F32).at[META_ROWS - n_meta:].set(meta_tokens.astype(F32))
    nothing = (jnp.zeros((8, c_rkv), F32), jnp.zeros((8, c_lora), F32))
    u_meta, mz_rkv, mz_lora, m_gate = mixer_inputs(meta, 1, META_ROWS, nothing)
    raw = lambda w: _matmul(u_meta, w, META_ROWS, _pick(w.shape[1], wide_tiles))
    ng = _pick(groups, (8, 4, 2, 1))
    _, s_meta = _rwkv(mz_rkv, mz_lora, m_gate, par, w2, a2, jnp.zeros((groups, GL, GL), F32), 1, META_ROWS, ng)

    xf = x.reshape(rows, d)
    tm = _pick(rows, (1024, 512, 256, 128, 64))
    m_conv = raw(w_conv)
    u, z_rkv, z_lora, gate = mixer_inputs(xf, bsz, _pick(seq, (1024, 512, 256, 128, 64)), (raw(w_rkv), raw(w_lora)))
    t_scan = _pick(seq, (256, 128, 64))
    ya, _ = _rwkv(z_rkv, z_lora, gate, par, w2, a2, s_meta.reshape(groups, GL, GL), bsz, t_scan, ng)
    yb = _conv_proj(u, w_conv, w_gb, conv_w[layer], m_conv, bsz, _pick(seq, (1024, 512, 256, 128, 64)),
                    _pick(d, (256, 128)))
    tq = _pick(rows, (512, 256, 128, 64))
    h1 = _out_proj(ya, yb, wo_a, wo_b, xf, tq, _pick(d, (512, 256, 128)))
    u2 = _rmsnorm(h1, norm_mlp_g[layer], BF16, min(tq, 256))
    big = (1024, 512, 256, 128)
    hid = _mlp_up(u2, wu, tm, _pick(wu.shape[1], big))
    h2 = _mlp_down(hid, wd, h1, tm, _pick(d, (512, 256, 128)), _pick(wu.shape[1], (4096, 2048, 1024, 512)))
    out = _rmsnorm(h2, norm_final_g, F32, min(tq, 256))
    return out.reshape(bsz, seq, d)
```

```python
import functools
import math

import jax
import jax.numpy as jnp
from jax import lax
from jax.experimental import pallas as pl
from jax.experimental.pallas import tpu as pltpu

F32 = jnp.float32
BF16 = jnp.bfloat16

HEAD = 64
GROUP = 4
GL = GROUP * HEAD
CHUNK = 64
LORA = 128
NORM_EPS = 1e-6
GN_EPS = 64e-5
LOG2_DECAY_SCALE = -math.exp(-0.5) / math.log(2.0)
META_ROWS = 64
VMEM_LIMIT = 56 * 1024 * 1024


def _params(*sem):
    return pltpu.CompilerParams(dimension_semantics=sem, vmem_limit_bytes=VMEM_LIMIT)


def _sigmoid(x):
    return 0.5 * jnp.tanh(0.5 * x) + 0.5


def _ones_bd():
    return (lax.broadcasted_iota(jnp.int32, (GL, GL), 0) // HEAD
            == lax.broadcasted_iota(jnp.int32, (GL, GL), 1) // HEAD).astype(BF16)


def _rmsnorm_kernel(x_ref, g_ref, o_ref):
    x = x_ref[...]
    y = x * lax.rsqrt(jnp.mean(x * x, axis=-1, keepdims=True) + NORM_EPS)
    o_ref[...] = (y * g_ref[...]).astype(o_ref.dtype)


def _rmsnorm(x, g, out_dtype, tm):
    rows, d = x.shape
    return pl.pallas_call(
        _rmsnorm_kernel,
        grid=(rows // tm,),
        in_specs=[pl.BlockSpec((tm, d), lambda i: (i, 0)), pl.BlockSpec((1, d), lambda i: (0, 0))],
        out_specs=pl.BlockSpec((tm, d), lambda i: (i, 0)),
        out_shape=jax.ShapeDtypeStruct((rows, d), out_dtype),
        compiler_params=_params("parallel"),
        name="rmsnorm",
    )(x, g.reshape(1, d))


def _matmul_kernel(x_ref, w_ref, o_ref, *, gate):
    p = jnp.dot(x_ref[...], w_ref[...], preferred_element_type=F32)
    o_ref[...] = _sigmoid(p) if gate else p


def _matmul(x, w, tm, tn, gate=False):
    rows, k = x.shape
    n = w.shape[1]
    return pl.pallas_call(
        functools.partial(_matmul_kernel, gate=gate),
        grid=(rows // tm, n // tn),
        in_specs=[pl.BlockSpec((tm, k), lambda i, j: (i, 0)), pl.BlockSpec((k, tn), lambda i, j: (0, j))],
        out_specs=pl.BlockSpec((tm, tn), lambda i, j: (i, j)),
        out_shape=jax.ShapeDtypeStruct((rows, n), F32),
        compiler_params=_params("parallel", "parallel"),
        name="in_proj",
    )(x, w)


def _shift_proj_kernel(x_ref, w_ref, mu_ref, m_ref, o_ref, carry, *, tile, tiles_per_seq):
    i = pl.program_id(1)

    @pl.when(i % tiles_per_seq == 0)
    def _():
        carry[...] = m_ref[...]

    p = jnp.dot(x_ref[...], w_ref[...], preferred_element_type=F32)
    row = lax.broadcasted_iota(jnp.int32, p.shape, 0)
    prev = jnp.where(row == 0, carry[7:8, :], pltpu.roll(p, 1, 0))
    carry[...] = p[tile - 8:tile, :]
    o_ref[...] = p + (prev - p) * mu_ref[0:1, :]


def _shift_proj(x, w, mu, p_before, batch, tm, tn):
    rows, k = x.shape
    n = w.shape[1]
    tiles_per_seq = rows // batch // tm
    m_blk = p_before.shape[0] // 8 - 1
    mu8 = jnp.zeros((8, n), F32).at[0].set(mu)
    return pl.pallas_call(
        functools.partial(_shift_proj_kernel, tile=tm, tiles_per_seq=tiles_per_seq),
        grid=(n // tn, rows // tm),
        in_specs=[pl.BlockSpec((tm, k), lambda j, i: (i, 0)), pl.BlockSpec((k, tn), lambda j, i: (0, j)),
                  pl.BlockSpec((8, tn), lambda j, i: (0, j)), pl.BlockSpec((8, tn), lambda j, i: (m_blk, j))],
        out_specs=pl.BlockSpec((tm, tn), lambda j, i: (i, j)),
        out_shape=jax.ShapeDtypeStruct((rows, n), F32),
        scratch_shapes=[pltpu.VMEM((8, tn), F32)],
        compiler_params=_params("parallel", "arbitrary"),
        name="shift_proj",
    )(x, w, mu8, p_before)


_K_MU, _K_W0, _K_A0, _K_KK, _K_KA = range(5)


def _key_proj_kernel(x_ref, w_ref, par_ref, m_ref, zl_ref, w2_ref, a2_ref, k_ref, kk_ref, ka_ref, lw_ref, carry,
                     *, tile, tiles_per_seq):
    i = pl.program_id(1)

    @pl.when(i % tiles_per_seq == 0)
    def _():
        carry[...] = m_ref[...]

    par = par_ref[...]
    prow = lambda r: par[r:r + 1, :]
    zw = jnp.tanh(zl_ref[:, :LORA]).astype(BF16)
    za = zl_ref[:, LORA:].astype(BF16)
    xw = jnp.dot(zw, w2_ref[...], preferred_element_type=F32)
    xa = jnp.dot(za, a2_ref[...], preferred_element_type=F32)
    p = jnp.dot(x_ref[...], w_ref[...], preferred_element_type=F32)
    lw_ref[...] = LOG2_DECAY_SCALE * _sigmoid(prow(_K_W0) + xw)
    a = _sigmoid(prow(_K_A0) + xa)
    row = lax.broadcasted_iota(jnp.int32, p.shape, 0)
    prev = jnp.where(row == 0, carry[7:8, :], pltpu.roll(p, 1, 0))
    carry[...] = p[tile - 8:tile, :]
    z = p + (prev - p) * prow(_K_MU)
    k_ref[...] = z * (1.0 + (a - 1.0) * prow(_K_KA))
    kk = z * prow(_K_KK)
    sq = (kk * kk).astype(BF16)
    ones_bd = _ones_bd()
    ss = jnp.concatenate([jnp.dot(sq[:, c:c + GL], ones_bd, preferred_element_type=F32)
                          for c in range(0, sq.shape[1], GL)], axis=1)
    kk = kk * lax.rsqrt(jnp.maximum(ss, 1e-24))
    kk_ref[...] = kk
    ka_ref[...] = kk * a


def _key_proj(x, w, par, p_before, z_lora, w2, a2, batch, tm, tn):
    rows, k = x.shape
    n = w.shape[1]
    tiles_per_seq = rows // batch // tm
    m_blk = p_before.shape[0] // 8 - 1
    out = pl.BlockSpec((tm, tn), lambda j, i: (i, j))
    return pl.pallas_call(
        functools.partial(_key_proj_kernel, tile=tm, tiles_per_seq=tiles_per_seq),
        grid=(n // tn, rows // tm),
        in_specs=[pl.BlockSpec((tm, k), lambda j, i: (i, 0)), pl.BlockSpec((k, tn), lambda j, i: (0, j)),
                  pl.BlockSpec((8, tn), lambda j, i: (0, j)), pl.BlockSpec((8, tn), lambda j, i: (m_blk, j)),
                  pl.BlockSpec((tm, 2 * LORA), lambda j, i: (i, 0)),
                  pl.BlockSpec((LORA, tn), lambda j, i: (0, j)), pl.BlockSpec((LORA, tn), lambda j, i: (0, j))],
        out_specs=[out] * 4,
        out_shape=[jax.ShapeDtypeStruct((rows, n), F32)] * 4,
        scratch_shapes=[pltpu.VMEM((8, tn), F32)],
        compiler_params=_params("parallel", "arbitrary"),
        name="key_proj",
    )(x, w, par, p_before, z_lora, w2, a2)


def _out_proj_kernel(a_ref, b_ref, wa_ref, wb_ref, h_ref, o_ref):
    acc = jnp.dot(a_ref[...], wa_ref[...], preferred_element_type=F32)
    acc += jnp.dot(b_ref[...], wb_ref[...], preferred_element_type=F32)
    o_ref[...] = h_ref[...] + acc


def _out_proj(ya, yb, wa, wb, h, tm, tn):
    rows, k = ya.shape
    n = wa.shape[1]
    return pl.pallas_call(
        _out_proj_kernel,
        grid=(rows // tm, n // tn),
        in_specs=[pl.BlockSpec((tm, k), lambda i, j: (i, 0)), pl.BlockSpec((tm, k), lambda i, j: (i, 0)),
                  pl.BlockSpec((k, tn), lambda i, j: (0, j)), pl.BlockSpec((k, tn), lambda i, j: (0, j)),
                  pl.BlockSpec((tm, tn), lambda i, j: (i, j))],
        out_specs=pl.BlockSpec((tm, tn), lambda i, j: (i, j)),
        out_shape=jax.ShapeDtypeStruct((rows, n), F32),
        compiler_params=_params("parallel", "parallel"),
        name="out_proj",
    )(ya, yb, wa, wb, h)


def _mlp_up_kernel(x_ref, w_ref, o_ref):
    hid = jnp.dot(x_ref[...], w_ref[...], preferred_element_type=F32)
    o_ref[...] = jnp.square(jnp.maximum(hid, 0.0)).astype(o_ref.dtype)


def _mlp_up(x, w, tm, tn):
    rows, k = x.shape
    n = w.shape[1]
    return pl.pallas_call(
        _mlp_up_kernel,
        grid=(rows // tm, n // tn),
        in_specs=[pl.BlockSpec((tm, k), lambda i, j: (i, 0)), pl.BlockSpec((k, tn), lambda i, j: (0, j))],
        out_specs=pl.BlockSpec((tm, tn), lambda i, j: (i, j)),
        out_shape=jax.ShapeDtypeStruct((rows, n), BF16),
        compiler_params=_params("parallel", "parallel"),
        name="mlp_up",
    )(x, w)


def _mlp_down_kernel(x_ref, w_ref, h_ref, o_ref):
    @pl.when(pl.program_id(2) == 0)
    def _():
        o_ref[...] = h_ref[...]

    o_ref[...] += jnp.dot(x_ref[...], w_ref[...], preferred_element_type=F32)


def _mlp_down(x, w, h, tm, tn, tk):
    rows, k = x.shape
    n = w.shape[1]
    return pl.pallas_call(
        _mlp_down_kernel,
        grid=(rows // tm, n // tn, k // tk),
        in_specs=[pl.BlockSpec((tm, tk), lambda i, j, q: (i, q)), pl.BlockSpec((tk, tn), lambda i, j, q: (q, j)),
                  pl.BlockSpec((tm, tn), lambda i, j, q: (i, j))],
        out_specs=pl.BlockSpec((tm, tn), lambda i, j, q: (i, j)),
        out_shape=jax.ShapeDtypeStruct((rows, n), F32),
        compiler_params=_params("parallel", "parallel", "arbitrary"),
        name="mlp_down",
    )(x, w, h)


_P_RK, _P_LNW, _P_LNB = range(3)
_P_ROWS = 8


def _rwkv_kernel(r_ref, k_ref, v_ref, kk_ref, ka_ref, lw_ref, gate_ref, par_ref, s0_ref,
                 y_ref, sout_ref, state, y_s, *, tile, ng):
    t_idx = pl.program_id(2)

    @pl.when(t_idx == 0)
    def _():
        state[...] = s0_ref[...]

    par = par_ref[0]
    prow = lambda i: par[i:i + 1, :]
    ones_bd = _ones_bd()

    def head_sum(xb):
        return jnp.concatenate([jnp.dot(xb[:, g * GL:(g + 1) * GL], ones_bd, preferred_element_type=F32)
                                for g in range(ng)], axis=1)

    ti = lax.broadcasted_iota(jnp.int32, (CHUNK, GL), 0)
    lane = lax.broadcasted_iota(jnp.int32, (CHUNK, GL), 1)
    si = lane % HEAD
    strict = si < ti
    incl = si <= ti
    eye = jnp.where(si == ti, 1.0, 0.0).astype(F32)
    head_of_lane = [lane // HEAD == h for h in range(GROUP)]
    bd_mask = (lax.broadcasted_iota(jnp.int32, (GL, GL), 0) // HEAD
               == lax.broadcasted_iota(jnp.int32, (GL, GL), 1) // HEAD)
    tri2 = (lax.broadcasted_iota(jnp.int32, (CHUNK, 2 * CHUNK), 1) % CHUNK
            <= lax.broadcasted_iota(jnp.int32, (CHUNK, 2 * CHUNK), 0)).astype(BF16)

    def bd(x):
        xb = x.astype(BF16)
        zero = jnp.zeros((), BF16)
        return jnp.concatenate([jnp.where(m, xb, zero) for m in head_of_lane], axis=0)

    def rob_mm(x, y):
        return jnp.dot(x.astype(BF16), bd(y), preferred_element_type=F32)

    def nt_dot(x, y):
        return lax.dot_general(x, y, (((1,), (1,)), ((), ())), preferred_element_type=F32)

    def group_chunk(rows, g):
        cols = slice(g * GL, (g + 1) * GL)
        rc, kc, vc, kkc, kka, lwc = (s[rows, cols] for s in (r_ref, k_ref, v_ref, kk_ref, ka_ref, lw_ref))
        hi = lwc.astype(BF16)
        lo = (lwc - hi.astype(F32)).astype(BF16)
        cum = jnp.dot(tri2, jnp.concatenate([hi, lo], axis=0), preferred_element_type=F32)
        yield
        cum_end = cum[CHUNK - 1:CHUNK, :]
        e_cum = jnp.exp2(cum)
        e_neg = jnp.exp2(-cum)
        e_prev = jnp.exp2(cum - lwc)
        e_end = jnp.exp2(cum_end - cum)
        al = -kkc * e_prev
        rt = rc * e_cum
        bt = kka * e_neg
        kt = kc * e_neg
        bh = kka * e_end
        kh = kc * e_end
        lr = jnp.concatenate([al, rt], axis=0).astype(BF16)
        x_b = nt_dot(lr, bd(bt))
        x_k = nt_dot(lr, bd(kt))
        s_old = state[g]
        ars = nt_dot(lr, s_old.astype(BF16))
        yield
        a_ab = jnp.where(strict, x_b[:CHUNK], 0.0)
        a_rb = jnp.where(incl, x_b[CHUNK:], 0.0)
        a_ak = jnp.where(strict, x_k[:CHUNK], 0.0)
        a_rk = jnp.where(incl, x_k[CHUNK:], 0.0)
        av = ars + rob_mm(jnp.concatenate([a_ak, a_rk], axis=0), vc)
        steps = int(math.log2(CHUNK))
        pw = rob_mm(a_ab, a_ab)
        tinv = eye + a_ab
        yield
        for _ in range(steps - 2):
            both = rob_mm(jnp.concatenate([pw, tinv], axis=0), pw)
            pw = both[:CHUNK]
            tinv = tinv + both[CHUNK:]
            yield
        tinv = tinv + rob_mm(tinv, pw)
        yield
        u = rob_mm(tinv, av[:CHUNK])
        yield
        y_s[rows, cols] = av[CHUNK:] + rob_mm(a_rb, u)
        uv = jnp.concatenate([u, vc], axis=0).astype(BF16)
        bk = jnp.concatenate([bh, kh], axis=0).astype(BF16)
        ds = lax.dot_general(uv, bk, (((0,), (0,)), ((), ())), preferred_element_type=F32)
        state[g] = s_old * jnp.exp2(cum_end) + jnp.where(bd_mask, ds, 0.0)

    def chunk_body(c, _):
        rows = pl.ds(pl.multiple_of(c * CHUNK, CHUNK), CHUNK)
        live = [group_chunk(rows, g) for g in range(ng)]
        while live:
            live = [gen for gen in live if next(gen, StopIteration) is not StopIteration]
        return 0

    lax.fori_loop(0, tile // CHUNK, chunk_body, 0)

    y = y_s[...]
    inv_n = 1.0 / HEAD
    y_hi = y.astype(BF16)
    y_lo = (y - y_hi.astype(F32)).astype(BF16)
    d = y - (head_sum(y_hi) + head_sum(y_lo)) * inv_n
    var = head_sum((d * d).astype(BF16)) * inv_n
    yn = d * lax.rsqrt(var + GN_EPS) * prow(_P_LNW) + prow(_P_LNB)
    bonus = head_sum((r_ref[...] * k_ref[...] * prow(_P_RK)).astype(BF16)) * v_ref[...]
    y_ref[...] = (gate_ref[...] * (yn + bonus)).astype(y_ref.dtype)

    @pl.when(t_idx == pl.num_programs(2) - 1)
    def _():
        sout_ref[0] = state[...]


def _rwkv(z_rv, keyed, gate, par, s0, batch, tile, ng):
    rows = z_rv.shape[0]
    d = z_rv.shape[1] // 2
    groups = d // GL
    sg = groups // ng
    width = ng * GL
    nt = rows // batch // tile
    wide = lambda arr: arr.reshape(sg, ng, arr.shape[1], GL).transpose(0, 2, 1, 3).reshape(sg, arr.shape[1], width)
    row_blk = lambda off: pl.BlockSpec((tile, width), lambda b, g, t: (b * nt + t, off + g))
    k, kk, ka, lw = keyed
    return pl.pallas_call(
        functools.partial(_rwkv_kernel, tile=tile, ng=ng),
        grid=(batch, sg, nt),
        in_specs=[row_blk(0), row_blk(0), row_blk(sg), row_blk(0), row_blk(0), row_blk(0), row_blk(0),
                  pl.BlockSpec((1, _P_ROWS, width), lambda b, g, t: (g, 0, 0)),
                  pl.BlockSpec((ng, GL, GL), lambda b, g, t: (g, 0, 0))],
        out_specs=[row_blk(0), pl.BlockSpec((1, ng, GL, GL), lambda b, g, t: (b * sg + g, 0, 0, 0))],
        out_shape=[jax.ShapeDtypeStruct((rows, d), BF16), jax.ShapeDtypeStruct((batch * sg, ng, GL, GL), F32)],
        scratch_shapes=[pltpu.VMEM((ng, GL, GL), F32), pltpu.VMEM((tile, width), F32)],
        compiler_params=_params("parallel", "parallel", "arbitrary"),
        name="rwkv7",
    )(z_rv, k, z_rv, kk, ka, lw, gate, wide(par), s0)


def _conv_proj_kernel(u_ref, wb_ref, wc_ref, wh_ref, wg_ref, cw_ref, mc_ref, mh_ref, o_ref, carry,
                      *, tile, tiles_per_seq):
    i = pl.program_id(1)

    @pl.when(i % tiles_per_seq == 0)
    def _():
        carry[...] = mc_ref[...] * mh_ref[...]

    x = u_ref[...]
    proj = lambda w_ref: jnp.dot(x, w_ref[...], preferred_element_type=F32)
    u = proj(wc_ref) * proj(wh_ref)
    row = lax.broadcasted_iota(jnp.int32, u.shape, 0)
    prev1 = jnp.where(row == 0, carry[7:8, :], pltpu.roll(u, 1, 0))
    prev2 = jnp.where(row == 0, carry[6:7, :], jnp.where(row == 1, carry[7:8, :], pltpu.roll(u, 2, 0)))
    carry[...] = u[tile - 8:tile, :]
    w = cw_ref[...]
    conv = w[0:1, :] * prev2 + w[1:2, :] * prev1 + w[2:3, :] * u
    o_ref[...] = (_sigmoid(proj(wg_ref)) * (proj(wb_ref) * conv)).astype(o_ref.dtype)


def _conv_proj(u, w_conv, w_gate, conv_w, p_conv_meta, batch, tile, tc):
    rows, k = u.shape
    d = w_gate.shape[1]
    nc = d // tc
    tiles_per_seq = rows // batch // tile
    meta_blk = p_conv_meta.shape[0] // 8 - 1
    wblk = lambda off: pl.BlockSpec((k, tc), lambda j, i: (0, off + j))
    w8 = jnp.zeros((8, d), F32).at[:conv_w.shape[0]].set(conv_w)
    return pl.pallas_call(
        functools.partial(_conv_proj_kernel, tile=tile, tiles_per_seq=tiles_per_seq),
        grid=(nc, rows // tile),
        in_specs=[pl.BlockSpec((tile, k), lambda j, i: (i, 0)),
                  wblk(0), wblk(nc), wblk(2 * nc), wblk(0),
                  pl.BlockSpec((8, tc), lambda j, i: (0, j)),
                  pl.BlockSpec((8, tc), lambda j, i: (meta_blk, nc + j)),
                  pl.BlockSpec((8, tc), lambda j, i: (meta_blk, 2 * nc + j))],
        out_specs=pl.BlockSpec((tile, tc), lambda j, i: (i, j)),
        out_shape=jax.ShapeDtypeStruct((rows, d), BF16),
        scratch_shapes=[pltpu.VMEM((8, tc), F32)],
        compiler_params=_params("parallel", "arbitrary"),
        name="conv_proj",
    )(u, w_conv, w_conv, w_conv, w_gate, w8, p_conv_meta, p_conv_meta)


def _pick(n, prefs):
    for p in prefs:
        if n % p == 0:
            return p
    raise ValueError(f"no tile for {n} in {prefs}")


def kernel(x, meta_tokens, norm_mix_g, w_in, rwkv_shift_mu, rwkv_w0, rwkv_w2, rwkv_a0, rwkv_a2, rwkv_k_k, rwkv_k_a, rwkv_r_k, rwkv_ln_w, rwkv_ln_b, conv_w, w_out, norm_mlp_g, w_up, w_down, norm_final_g):
    bsz, seq, d = x.shape
    n_meta = meta_tokens.shape[0]
    depth = w_in.shape[0]
    assert depth == 1, "the meta-token hand-off is written for a single layer"
    assert d % GL == 0 and seq % CHUNK == 0 and n_meta <= META_ROWS
    groups = d // GL
    rows = bsz * seq
    layer = 0

    c_rkv, c_lora = 3 * d, 2 * LORA
    c0 = c_rkv + c_lora
    w_all = w_in[layer]
    w_rv = jnp.concatenate([w_all[:, :d], w_all[:, 2 * d:c_rkv]], axis=1).astype(BF16)
    w_k = w_all[:, d:2 * d].astype(BF16)
    w_lora = w_all[:, c_rkv:c0].astype(BF16)
    w_conv = w_all[:, c0:c0 + 3 * d].astype(BF16)
    w_ga = w_all[:, c0 + 3 * d:c0 + 4 * d].astype(BF16)
    w_gb = w_all[:, c0 + 4 * d:].astype(BF16)
    wo_a = w_out[layer, :d].astype(BF16)
    wo_b = w_out[layer, d:].astype(BF16)
    wu = w_up[layer].astype(BF16)
    wd = w_down[layer].astype(BF16)
    w2 = rwkv_w2[layer].astype(BF16)
    a2 = rwkv_a2[layer].astype(BF16)

    mu = rwkv_shift_mu[layer]
    mu_rv = jnp.concatenate([mu[:d], mu[2 * d:c_rkv]])
    rows8 = lambda vecs: jnp.zeros((8, d), F32).at[:len(vecs)].set(jnp.stack(vecs))
    key_par = rows8([mu[d:2 * d], rwkv_w0[layer], rwkv_a0[layer], rwkv_k_k[layer], rwkv_k_a[layer]])
    per_group = lambda vec: vec.reshape(groups, 1, GL)
    par = jnp.concatenate(
        [per_group(rwkv_r_k[layer].reshape(d)), per_group(rwkv_ln_w[layer]), per_group(rwkv_ln_b[layer]),
         jnp.zeros((groups, _P_ROWS - 3, GL), F32)], axis=1)
    wide_tiles = (1024, 512, 256)

    def mixer_inputs(h_rows, batch, tm, before):
        u = _rmsnorm(h_rows, norm_mix_g[layer], BF16, min(tm, 256))
        z_lora = _shift_proj(u, w_lora, mu[c_rkv:c0], before[0], batch, tm, c_lora)
        z_rv = _shift_proj(u, w_rv, mu_rv, before[1], batch, tm, _pick(d, wide_tiles))
        keyed = _key_proj(u, w_k, key_par, before[2], z_lora, w2, a2, batch, tm, _pick(d, (512, 256)))
        return u, z_rv, keyed, _matmul(u, w_ga, tm, _pick(d, wide_tiles), gate=True)

    meta = jnp.zeros((META_ROWS, d), F32).at[META_ROWS - n_meta:].set(meta_tokens.astype(F32))
    nothing = (jnp.zeros((8, c_lora), F32), jnp.zeros((8, 2 * d), F32), jnp.zeros((8, d), F32))
    u_meta, mz_rv, m_keyed, m_gate = mixer_inputs(meta, 1, META_ROWS, nothing)
    raw = lambda w: _matmul(u_meta, w, META_ROWS, _pick(w.shape[1], wide_tiles))
    ng = _pick(groups, (8, 4, 2, 1))
    _, s_meta = _rwkv(mz_rv, m_keyed, m_gate, par, jnp.zeros((groups, GL, GL), F32), 1, META_ROWS, ng)

    xf = x.reshape(rows, d)
    tm = _pick(rows, (1024, 512, 256, 128, 64))
    m_conv = raw(w_conv)
    u, z_rv, keyed, gate = mixer_inputs(xf, bsz, _pick(seq, (1024, 512, 256, 128, 64)),
                                        (raw(w_lora), raw(w_rv), raw(w_k)))
    t_scan = _pick(seq, (256, 128, 64))
    ya, _ = _rwkv(z_rv, keyed, gate, par, s_meta.reshape(groups, GL, GL), bsz, t_scan, ng)
    yb = _conv_proj(u, w_conv, w_gb, conv_w[layer], m_conv, bsz, _pick(seq, (1024, 512, 256, 128, 64)),
                    _pick(d, (256, 128)))
    tq = _pick(rows, (512, 256, 128, 64))
    h1 = _out_proj(ya, yb, wo_a, wo_b, xf, tq, _pick(d, (512, 256, 128)))
    u2 = _rmsnorm(h1, norm_mlp_g[layer], BF16, min(tq, 256))
    big = (1024, 512, 256, 128)
    hid = _mlp_up(u2, wu, tm, _pick(wu.shape[1], big))
    h2 = _mlp_down(hid, wd, h1, tm, _pick(d, big), _pick(wu.shape[1], (4096, 2048, 1024, 512)))
    out = _rmsnorm(h2, norm_final_g, F32, min(tq, 256))
    return out.reshape(bsz, seq, d)
```

```python
import functools
import math

import jax
import jax.numpy as jnp
from jax import lax
from jax.experimental import pallas as pl
from jax.experimental.pallas import tpu as pltpu

F32 = jnp.float32
BF16 = jnp.bfloat16

HEAD = 64
GROUP = 4
GL = GROUP * HEAD
CHUNK = 64
LORA = 128
NORM_EPS = 1e-6
GN_EPS = 64e-5
LOG2_DECAY_SCALE = -math.exp(-0.5) / math.log(2.0)
META_ROWS = 64
VMEM_LIMIT = 56 * 1024 * 1024


def _params(*sem):
    return pltpu.CompilerParams(dimension_semantics=sem, vmem_limit_bytes=VMEM_LIMIT)


def _sigmoid(x):
    return 0.5 * jnp.tanh(0.5 * x) + 0.5


def _ones_bd():
    return (lax.broadcasted_iota(jnp.int32, (GL, GL), 0) // HEAD
            == lax.broadcasted_iota(jnp.int32, (GL, GL), 1) // HEAD).astype(BF16)


def _rmsnorm_kernel(x_ref, g_ref, o_ref):
    x = x_ref[...]
    y = x * lax.rsqrt(jnp.mean(x * x, axis=-1, keepdims=True) + NORM_EPS)
    o_ref[...] = (y * g_ref[...]).astype(o_ref.dtype)


def _rmsnorm(x, g, out_dtype, tm):
    rows, d = x.shape
    return pl.pallas_call(
        _rmsnorm_kernel,
        grid=(rows // tm,),
        in_specs=[pl.BlockSpec((tm, d), lambda i: (i, 0)), pl.BlockSpec((1, d), lambda i: (0, 0))],
        out_specs=pl.BlockSpec((tm, d), lambda i: (i, 0)),
        out_shape=jax.ShapeDtypeStruct((rows, d), out_dtype),
        compiler_params=_params("parallel"),
        name="rmsnorm",
    )(x, g.reshape(1, d))


def _matmul_kernel(x_ref, w_ref, o_ref, *, gate):
    p = jnp.dot(x_ref[...], w_ref[...], preferred_element_type=F32)
    o_ref[...] = _sigmoid(p) if gate else p


def _matmul(x, w, tm, tn, gate=False):
    rows, k = x.shape
    n = w.shape[1]
    return pl.pallas_call(
        functools.partial(_matmul_kernel, gate=gate),
        grid=(rows // tm, n // tn),
        in_specs=[pl.BlockSpec((tm, k), lambda i, j: (i, 0)), pl.BlockSpec((k, tn), lambda i, j: (0, j))],
        out_specs=pl.BlockSpec((tm, tn), lambda i, j: (i, j)),
        out_shape=jax.ShapeDtypeStruct((rows, n), F32),
        compiler_params=_params("parallel", "parallel"),
        name="in_proj",
    )(x, w)


def _shift_proj_kernel(x_ref, w_ref, mu_ref, m_ref, o_ref, carry, *, tile, tiles_per_seq):
    i = pl.program_id(1)

    @pl.when(i % tiles_per_seq == 0)
    def _():
        carry[...] = m_ref[...]

    p = jnp.dot(x_ref[...], w_ref[...], preferred_element_type=F32)
    row = lax.broadcasted_iota(jnp.int32, p.shape, 0)
    prev = jnp.where(row == 0, carry[7:8, :], pltpu.roll(p, 1, 0))
    carry[...] = p[tile - 8:tile, :]
    o_ref[...] = p + (prev - p) * mu_ref[0:1, :]


def _shift_proj(x, w, col_blk, mu, p_before, batch, tm, tn):
    rows, k = x.shape
    n = mu.shape[0]
    tiles_per_seq = rows // batch // tm
    m_blk = p_before.shape[0] // 8 - 1
    mu8 = jnp.zeros((8, n), F32).at[0].set(mu)
    return pl.pallas_call(
        functools.partial(_shift_proj_kernel, tile=tm, tiles_per_seq=tiles_per_seq),
        grid=(n // tn, rows // tm),
        in_specs=[pl.BlockSpec((tm, k), lambda j, i: (i, 0)), pl.BlockSpec((k, tn), lambda j, i: (0, col_blk(j))),
                  pl.BlockSpec((8, tn), lambda j, i: (0, j)), pl.BlockSpec((8, tn), lambda j, i: (m_blk, j))],
        out_specs=pl.BlockSpec((tm, tn), lambda j, i: (i, j)),
        out_shape=jax.ShapeDtypeStruct((rows, n), F32),
        scratch_shapes=[pltpu.VMEM((8, tn), F32)],
        compiler_params=_params("parallel", "arbitrary"),
        name="shift_proj",
    )(x, w, mu8, p_before)


_K_MU, _K_W0, _K_A0, _K_KK, _K_KA = range(5)


def _key_proj_kernel(x_ref, w_ref, par_ref, m_ref, zl_ref, w2_ref, a2_ref, k_ref, kk_ref, ka_ref, lw_ref, carry,
                     *, tile, tiles_per_seq):
    i = pl.program_id(1)

    @pl.when(i % tiles_per_seq == 0)
    def _():
        carry[...] = m_ref[...]

    par = par_ref[...]
    prow = lambda r: par[r:r + 1, :]
    zw = jnp.tanh(zl_ref[:, :LORA]).astype(BF16)
    za = zl_ref[:, LORA:].astype(BF16)
    xw = jnp.dot(zw, w2_ref[...], preferred_element_type=F32)
    xa = jnp.dot(za, a2_ref[...], preferred_element_type=F32)
    p = jnp.dot(x_ref[...], w_ref[...], preferred_element_type=F32)
    lw_ref[...] = LOG2_DECAY_SCALE * _sigmoid(prow(_K_W0) + xw)
    a = _sigmoid(prow(_K_A0) + xa)
    row = lax.broadcasted_iota(jnp.int32, p.shape, 0)
    prev = jnp.where(row == 0, carry[7:8, :], pltpu.roll(p, 1, 0))
    carry[...] = p[tile - 8:tile, :]
    z = p + (prev - p) * prow(_K_MU)
    k_ref[...] = z * (1.0 + (a - 1.0) * prow(_K_KA))
    kk = z * prow(_K_KK)
    sq = (kk * kk).astype(BF16)
    ones_bd = _ones_bd()
    ss = jnp.concatenate([jnp.dot(sq[:, c:c + GL], ones_bd, preferred_element_type=F32)
                          for c in range(0, sq.shape[1], GL)], axis=1)
    kk = kk * lax.rsqrt(jnp.maximum(ss, 1e-24))
    kk_ref[...] = kk
    ka_ref[...] = kk * a


def _key_proj(x, w, col_blk, par, p_before, z_lora, w2, a2, batch, tm, tn):
    rows, k = x.shape
    n = par.shape[1]
    tiles_per_seq = rows // batch // tm
    m_blk = p_before.shape[0] // 8 - 1
    out = pl.BlockSpec((tm, tn), lambda j, i: (i, j))
    return pl.pallas_call(
        functools.partial(_key_proj_kernel, tile=tm, tiles_per_seq=tiles_per_seq),
        grid=(n // tn, rows // tm),
        in_specs=[pl.BlockSpec((tm, k), lambda j, i: (i, 0)), pl.BlockSpec((k, tn), lambda j, i: (0, col_blk(j))),
                  pl.BlockSpec((8, tn), lambda j, i: (0, j)), pl.BlockSpec((8, tn), lambda j, i: (m_blk, j)),
                  pl.BlockSpec((tm, 2 * LORA), lambda j, i: (i, 0)),
                  pl.BlockSpec((LORA, tn), lambda j, i: (0, j)), pl.BlockSpec((LORA, tn), lambda j, i: (0, j))],
        out_specs=[out] * 4,
        out_shape=[jax.ShapeDtypeStruct((rows, n), F32)] * 4,
        scratch_shapes=[pltpu.VMEM((8, tn), F32)],
        compiler_params=_params("parallel", "arbitrary"),
        name="key_proj",
    )(x, w, par, p_before, z_lora, w2, a2)


def _out_proj_kernel(a_ref, b_ref, wa_ref, wb_ref, h_ref, o_ref):
    acc = jnp.dot(a_ref[...], wa_ref[...], preferred_element_type=F32)
    acc += jnp.dot(b_ref[...], wb_ref[...], preferred_element_type=F32)
    o_ref[...] = h_ref[...] + acc


def _out_proj(ya, yb, wa, wb, h, tm, tn):
    rows, k = ya.shape
    n = wa.shape[1]
    return pl.pallas_call(
        _out_proj_kernel,
        grid=(rows // tm, n // tn),
        in_specs=[pl.BlockSpec((tm, k), lambda i, j: (i, 0)), pl.BlockSpec((tm, k), lambda i, j: (i, 0)),
                  pl.BlockSpec((k, tn), lambda i, j: (0, j)), pl.BlockSpec((k, tn), lambda i, j: (0, j)),
                  pl.BlockSpec((tm, tn), lambda i, j: (i, j))],
        out_specs=pl.BlockSpec((tm, tn), lambda i, j: (i, j)),
        out_shape=jax.ShapeDtypeStruct((rows, n), F32),
        compiler_params=_params("parallel", "parallel"),
        name="out_proj",
    )(ya, yb, wa, wb, h)


def _mlp_up_kernel(x_ref, w_ref, o_ref):
    hid = jnp.dot(x_ref[...], w_ref[...], preferred_element_type=F32)
    o_ref[...] = jnp.square(jnp.maximum(hid, 0.0)).astype(o_ref.dtype)


def _mlp_up(x, w, tm, tn):
    rows, k = x.shape
    n = w.shape[1]
    return pl.pallas_call(
        _mlp_up_kernel,
        grid=(rows // tm, n // tn),
        in_specs=[pl.BlockSpec((tm, k), lambda i, j: (i, 0)), pl.BlockSpec((k, tn), lambda i, j: (0, j))],
        out_specs=pl.BlockSpec((tm, tn), lambda i, j: (i, j)),
        out_shape=jax.ShapeDtypeStruct((rows, n), BF16),
        compiler_params=_params("parallel", "parallel"),
        name="mlp_up",
    )(x, w)


def _mlp_down_kernel(x_ref, w_ref, h_ref, o_ref):
    @pl.when(pl.program_id(2) == 0)
    def _():
        o_ref[...] = h_ref[...]

    o_ref[...] += jnp.dot(x_ref[...], w_ref[...], preferred_element_type=F32)


def _mlp_down(x, w, h, tm, tn, tk):
    rows, k = x.shape
    n = w.shape[1]
    return pl.pallas_call(
        _mlp_down_kernel,
        grid=(rows // tm, n // tn, k // tk),
        in_specs=[pl.BlockSpec((tm, tk), lambda i, j, q: (i, q)), pl.BlockSpec((tk, tn), lambda i, j, q: (q, j)),
                  pl.BlockSpec((tm, tn), lambda i, j, q: (i, j))],
        out_specs=pl.BlockSpec((tm, tn), lambda i, j, q: (i, j)),
        out_shape=jax.ShapeDtypeStruct((rows, n), F32),
        compiler_params=_params("parallel", "parallel", "arbitrary"),
        name="mlp_down",
    )(x, w, h)


_P_RK, _P_LNW, _P_LNB = range(3)
_P_ROWS = 8


def _rwkv_kernel(r_ref, k_ref, v_ref, kk_ref, ka_ref, lw_ref, gate_ref, par_ref, s0_ref,
                 y_ref, sout_ref, state, y_s, *, tile, ng):
    t_idx = pl.program_id(2)

    @pl.when(t_idx == 0)
    def _():
        state[...] = s0_ref[...]

    par = par_ref[0]
    prow = lambda i: par[i:i + 1, :]
    ones_bd = _ones_bd()

    def head_sum(xb):
        return jnp.concatenate([jnp.dot(xb[:, g * GL:(g + 1) * GL], ones_bd, preferred_element_type=F32)
                                for g in range(ng)], axis=1)

    ti = lax.broadcasted_iota(jnp.int32, (CHUNK, GL), 0)
    lane = lax.broadcasted_iota(jnp.int32, (CHUNK, GL), 1)
    si = lane % HEAD
    strict = si < ti
    incl = si <= ti
    eye = jnp.where(si == ti, 1.0, 0.0).astype(F32)
    head_of_lane = [lane // HEAD == h for h in range(GROUP)]
    bd_mask = (lax.broadcasted_iota(jnp.int32, (GL, GL), 0) // HEAD
               == lax.broadcasted_iota(jnp.int32, (GL, GL), 1) // HEAD)
    tri2 = (lax.broadcasted_iota(jnp.int32, (CHUNK, 2 * CHUNK), 1) % CHUNK
            <= lax.broadcasted_iota(jnp.int32, (CHUNK, 2 * CHUNK), 0)).astype(BF16)

    def bd(x):
        xb = x.astype(BF16)
        zero = jnp.zeros((), BF16)
        return jnp.concatenate([jnp.where(m, xb, zero) for m in head_of_lane], axis=0)

    def rob_mm(x, y):
        return jnp.dot(x.astype(BF16), bd(y), preferred_element_type=F32)

    def nt_dot(x, y):
        return lax.dot_general(x, y, (((1,), (1,)), ((), ())), preferred_element_type=F32)

    def group_chunk(rows, g):
        cols = slice(g * GL, (g + 1) * GL)
        rc, kc, vc, kkc, kka, lwc = (s[rows, cols] for s in (r_ref, k_ref, v_ref, kk_ref, ka_ref, lw_ref))
        hi = lwc.astype(BF16)
        lo = (lwc - hi.astype(F32)).astype(BF16)
        cum = jnp.dot(tri2, jnp.concatenate([hi, lo], axis=0), preferred_element_type=F32)
        yield
        cum_end = cum[CHUNK - 1:CHUNK, :]
        e_cum = jnp.exp2(cum)
        e_neg = jnp.exp2(-cum)
        e_prev = jnp.exp2(cum - lwc)
        e_end = jnp.exp2(cum_end - cum)
        al = -kkc * e_prev
        rt = rc * e_cum
        bt = kka * e_neg
        kt = kc * e_neg
        bh = kka * e_end
        kh = kc * e_end
        lr = jnp.concatenate([al, rt], axis=0).astype(BF16)
        x_b = nt_dot(lr, bd(bt))
        x_k = nt_dot(lr, bd(kt))
        s_old = state[g]
        ars = nt_dot(lr, s_old.astype(BF16))
        yield
        a_ab = jnp.where(strict, x_b[:CHUNK], 0.0)
        a_rb = jnp.where(incl, x_b[CHUNK:], 0.0)
        a_ak = jnp.where(strict, x_k[:CHUNK], 0.0)
        a_rk = jnp.where(incl, x_k[CHUNK:], 0.0)
        av = ars + rob_mm(jnp.concatenate([a_ak, a_rk], axis=0), vc)
        steps = int(math.log2(CHUNK))
        pw = rob_mm(a_ab, a_ab)
        tinv = eye + a_ab
        yield
        for _ in range(steps - 2):
            both = rob_mm(jnp.concatenate([pw, tinv], axis=0), pw)
            pw = both[:CHUNK]
            tinv = tinv + both[CHUNK:]
            yield
        tinv = tinv + rob_mm(tinv, pw)
        yield
        u = rob_mm(tinv, av[:CHUNK])
        yield
        y_s[rows, cols] = av[CHUNK:] + rob_mm(a_rb, u)
        uv = jnp.concatenate([u, vc], axis=0).astype(BF16)
        bk = jnp.concatenate([bh, kh], axis=0).astype(BF16)
        ds = lax.dot_general(uv, bk, (((0,), (0,)), ((), ())), preferred_element_type=F32)
        state[g] = s_old * jnp.exp2(cum_end) + jnp.where(bd_mask, ds, 0.0)

    def chunk_body(c, _):
        rows = pl.ds(pl.multiple_of(c * CHUNK, CHUNK), CHUNK)
        live = [group_chunk(rows, g) for g in range(ng)]
        while live:
            live = [gen for gen in live if next(gen, StopIteration) is not StopIteration]
        return 0

    lax.fori_loop(0, tile // CHUNK, chunk_body, 0)

    y = y_s[...]
    inv_n = 1.0 / HEAD
    y_hi = y.astype(BF16)
    y_lo = (y - y_hi.astype(F32)).astype(BF16)
    d = y - (head_sum(y_hi) + head_sum(y_lo)) * inv_n
    var = head_sum((d * d).astype(BF16)) * inv_n
    yn = d * lax.rsqrt(var + GN_EPS) * prow(_P_LNW) + prow(_P_LNB)
    bonus = head_sum((r_ref[...] * k_ref[...] * prow(_P_RK)).astype(BF16)) * v_ref[...]
    y_ref[...] = (gate_ref[...] * (yn + bonus)).astype(y_ref.dtype)

    @pl.when(t_idx == pl.num_programs(2) - 1)
    def _():
        sout_ref[0] = state[...]


def _rwkv(z_rv, keyed, gate, par, s0, batch, tile, ng):
    rows = z_rv.shape[0]
    d = z_rv.shape[1] // 2
    groups = d // GL
    sg = groups // ng
    width = ng * GL
    nt = rows // batch // tile
    wide = lambda arr: arr.reshape(sg, ng, arr.shape[1], GL).transpose(0, 2, 1, 3).reshape(sg, arr.shape[1], width)
    row_blk = lambda off: pl.BlockSpec((tile, width), lambda b, g, t: (b * nt + t, off + g))
    k, kk, ka, lw = keyed
    return pl.pallas_call(
        functools.partial(_rwkv_kernel, tile=tile, ng=ng),
        grid=(batch, sg, nt),
        in_specs=[row_blk(0), row_blk(0), row_blk(sg), row_blk(0), row_blk(0), row_blk(0), row_blk(0),
                  pl.BlockSpec((1, _P_ROWS, width), lambda b, g, t: (g, 0, 0)),
                  pl.BlockSpec((ng, GL, GL), lambda b, g, t: (g, 0, 0))],
        out_specs=[row_blk(0), pl.BlockSpec((1, ng, GL, GL), lambda b, g, t: (b * sg + g, 0, 0, 0))],
        out_shape=[jax.ShapeDtypeStruct((rows, d), BF16), jax.ShapeDtypeStruct((batch * sg, ng, GL, GL), F32)],
        scratch_shapes=[pltpu.VMEM((ng, GL, GL), F32), pltpu.VMEM((tile, width), F32)],
        compiler_params=_params("parallel", "parallel", "arbitrary"),
        name="rwkv7",
    )(z_rv, k, z_rv, kk, ka, lw, gate, wide(par), s0)


def _conv_proj_kernel(u_ref, wb_ref, wc_ref, wh_ref, wg_ref, cw_ref, mc_ref, mh_ref, o_ref, carry,
                      *, tile, tiles_per_seq):
    i = pl.program_id(1)

    @pl.when(i % tiles_per_seq == 0)
    def _():
        carry[...] = mc_ref[...] * mh_ref[...]

    x = u_ref[...]
    proj = lambda w_ref: jnp.dot(x, w_ref[...], preferred_element_type=F32)
    u = proj(wc_ref) * proj(wh_ref)
    row = lax.broadcasted_iota(jnp.int32, u.shape, 0)
    prev1 = jnp.where(row == 0, carry[7:8, :], pltpu.roll(u, 1, 0))
    prev2 = jnp.where(row == 0, carry[6:7, :], jnp.where(row == 1, carry[7:8, :], pltpu.roll(u, 2, 0)))
    carry[...] = u[tile - 8:tile, :]
    w = cw_ref[...]
    conv = w[0:1, :] * prev2 + w[1:2, :] * prev1 + w[2:3, :] * u
    o_ref[...] = (_sigmoid(proj(wg_ref)) * (proj(wb_ref) * conv)).astype(o_ref.dtype)


def _conv_proj(u, w, col0, gate_col0, conv_w, p_conv_meta, batch, tile, tc):
    rows, k = u.shape
    d = conv_w.shape[1]
    nc = d // tc
    tiles_per_seq = rows // batch // tile
    meta_blk = p_conv_meta.shape[0] // 8 - 1
    wblk = lambda col: pl.BlockSpec((k, tc), lambda j, i: (0, col // tc + j))
    w8 = jnp.zeros((8, d), F32).at[:conv_w.shape[0]].set(conv_w)
    return pl.pallas_call(
        functools.partial(_conv_proj_kernel, tile=tile, tiles_per_seq=tiles_per_seq),
        grid=(nc, rows // tile),
        in_specs=[pl.BlockSpec((tile, k), lambda j, i: (i, 0)),
                  wblk(col0), wblk(col0 + d), wblk(col0 + 2 * d), wblk(gate_col0),
                  pl.BlockSpec((8, tc), lambda j, i: (0, j)),
                  pl.BlockSpec((8, tc), lambda j, i: (meta_blk, nc + j)),
                  pl.BlockSpec((8, tc), lambda j, i: (meta_blk, 2 * nc + j))],
        out_specs=pl.BlockSpec((tile, tc), lambda j, i: (i, j)),
        out_shape=jax.ShapeDtypeStruct((rows, d), BF16),
        scratch_shapes=[pltpu.VMEM((8, tc), F32)],
        compiler_params=_params("parallel", "arbitrary"),
        name="conv_proj",
    )(u, w, w, w, w, w8, p_conv_meta, p_conv_meta)


def _pick(n, prefs):
    for p in prefs:
        if n % p == 0:
            return p
    raise ValueError(f"no tile for {n} in {prefs}")


def kernel(x, meta_tokens, norm_mix_g, w_in, rwkv_shift_mu, rwkv_w0, rwkv_w2, rwkv_a0, rwkv_a2, rwkv_k_k, rwkv_k_a, rwkv_r_k, rwkv_ln_w, rwkv_ln_b, conv_w, w_out, norm_mlp_g, w_up, w_down, norm_final_g):
    bsz, seq, d = x.shape
    n_meta = meta_tokens.shape[0]
    depth = w_in.shape[0]
    assert depth == 1, "the meta-token hand-off is written for a single layer"
    assert d % GL == 0 and seq % CHUNK == 0 and n_meta <= META_ROWS
    groups = d // GL
    rows = bsz * seq
    layer = 0

    c_rkv, c_lora = 3 * d, 2 * LORA
    c0 = c_rkv + c_lora
    c_ga = c0 + 3 * d
    tc = _pick(d, (256, 128))
    assert c_rkv % c_lora == 0 and c0 % tc == 0, "column segments must start on block boundaries"
    w_all = w_in[layer].astype(BF16)
    w_ga = w_all[:, c_ga:c_ga + d]
    wo_a = w_out[layer, :d].astype(BF16)
    wo_b = w_out[layer, d:].astype(BF16)
    wu = w_up[layer].astype(BF16)
    wd = w_down[layer].astype(BF16)
    w2 = rwkv_w2[layer].astype(BF16)
    a2 = rwkv_a2[layer].astype(BF16)

    mu = rwkv_shift_mu[layer]
    mu_rv = jnp.concatenate([mu[:d], mu[2 * d:c_rkv]])
    rows8 = lambda vecs: jnp.zeros((8, d), F32).at[:len(vecs)].set(jnp.stack(vecs))
    key_par = rows8([mu[d:2 * d], rwkv_w0[layer], rwkv_a0[layer], rwkv_k_k[layer], rwkv_k_a[layer]])
    per_group = lambda vec: vec.reshape(groups, 1, GL)
    par = jnp.concatenate(
        [per_group(rwkv_r_k[layer].reshape(d)), per_group(rwkv_ln_w[layer]), per_group(rwkv_ln_b[layer]),
         jnp.zeros((groups, _P_ROWS - 3, GL), F32)], axis=1)
    tw = _pick(d, (1024, 512, 256))
    tk = _pick(d, (512, 256))
    rv_blk = lambda j: j + (j // (d // tw)) * (d // tw)
    k_blk = lambda j: d // tk + j
    lora_blk = lambda j: c_rkv // c_lora + j

    def mixer_inputs(h_rows, batch, tm, before):
        u = _rmsnorm(h_rows, norm_mix_g[layer], BF16, min(tm, 256))
        z_lora = _shift_proj(u, w_all, lora_blk, mu[c_rkv:c0], before[0], batch, tm, c_lora)
        z_rv = _shift_proj(u, w_all, rv_blk, mu_rv, before[1], batch, tm, tw)
        keyed = _key_proj(u, w_all, k_blk, key_par, before[2], z_lora, w2, a2, batch, tm, tk)
        return u, z_rv, keyed, _matmul(u, w_ga, tm, _pick(d, (1024, 512, 256)), gate=True)

    meta = jnp.zeros((META_ROWS, d), F32).at[META_ROWS - n_meta:].set(meta_tokens.astype(F32))
    nothing = (jnp.zeros((8, c_lora), F32), jnp.zeros((8, 2 * d), F32), jnp.zeros((8, d), F32))
    u_meta, mz_rv, m_keyed, m_gate = mixer_inputs(meta, 1, META_ROWS, nothing)

    def raw(col_blk, n, tn):
        return _shift_proj(u_meta, w_all, col_blk, jnp.zeros((n,), F32), jnp.zeros((8, n), F32), 1, META_ROWS, tn)

    ng = _pick(groups, (8, 4, 2, 1))
    _, s_meta = _rwkv(mz_rv, m_keyed, m_gate, par, jnp.zeros((groups, GL, GL), F32), 1, META_ROWS, ng)

    xf = x.reshape(rows, d)
    tm = _pick(rows, (1024, 512, 256, 128, 64))
    t_seq = _pick(seq, (1024, 512, 256, 128, 64))
    m_conv = raw(lambda j: c0 // tc + j, 3 * d, tc)
    u, z_rv, keyed, gate = mixer_inputs(xf, bsz, t_seq, (raw(lora_blk, c_lora, c_lora), raw(rv_blk, 2 * d, tw),
                                                         raw(k_blk, d, tk)))
    t_scan = _pick(seq, (256, 128, 64))
    ya, _ = _rwkv(z_rv, keyed, gate, par, s_meta.reshape(groups, GL, GL), bsz, t_scan, ng)
    yb = _conv_proj(u, w_all, c0, c_ga + d, conv_w[layer], m_conv, bsz, t_seq, tc)
    tq = _pick(rows, (512, 256, 128, 64))
    h1 = _out_proj(ya, yb, wo_a, wo_b, xf, tq, _pick(d, (512, 256, 128)))
    u2 = _rmsnorm(h1, norm_mlp_g[layer], BF16, min(tq, 256))
    big = (1024, 512, 256, 128)
    hid = _mlp_up(u2, wu, tm, _pick(wu.shape[1], big))
    h2 = _mlp_down(hid, wd, h1, tm, _pick(d, big), _pick(wu.shape[1], (4096, 2048, 1024, 512)))
    out = _rmsnorm(h2, norm_final_g, F32, min(tq, 256))
    return out.reshape(bsz, seq, d)
```

```python
import functools
import math

import jax
import jax.numpy as jnp
from jax import lax
from jax.experimental import pallas as pl
from jax.experimental.pallas import tpu as pltpu

F32 = jnp.float32
BF16 = jnp.bfloat16

HEAD = 64
GROUP = 4
GL = GROUP * HEAD
CHUNK = 64
LORA = 128
NORM_EPS = 1e-6
GN_EPS = 64e-5
LOG2_DECAY_SCALE = -math.exp(-0.5) / math.log(2.0)
META_ROWS = 64
VMEM_LIMIT = 56 * 1024 * 1024


def _params(*sem):
    return pltpu.CompilerParams(dimension_semantics=sem, vmem_limit_bytes=VMEM_LIMIT)


def _sigmoid(x):
    return 0.5 * jnp.tanh(0.5 * x) + 0.5


def _ones_bd():
    return (lax.broadcasted_iota(jnp.int32, (GL, GL), 0) // HEAD
            == lax.broadcasted_iota(jnp.int32, (GL, GL), 1) // HEAD).astype(BF16)


def _with_cast(body, n_in, n_out):
    def kernel(*refs):
        src, dst = refs[n_in], refs[n_in + 1 + n_out]
        dst[...] = src[...].astype(dst.dtype)
        body(*refs[:n_in], *refs[n_in + 1:n_in + 1 + n_out], *refs[n_in + n_out + 2:])

    return kernel


def _cast_rows(src, steps):
    chunk = src.shape[0] // steps
    return chunk if src.shape[0] % steps == 0 and chunk % 16 == 0 else None


def _proj_call(body, n_col, n_row, in_specs, out_specs, out_shape, scratch, name, args, cast):
    if cast is not None:
        spec = pl.BlockSpec((_cast_rows(cast, n_col * n_row), cast.shape[1]), lambda j, i: (j * n_row + i, 0))
        body = _with_cast(body, len(in_specs), len(out_specs))
        in_specs, out_specs = [*in_specs, spec], [*out_specs, spec]
        out_shape, args = [*out_shape, jax.ShapeDtypeStruct(cast.shape, BF16)], (*args, cast)
    return pl.pallas_call(body, grid=(n_col, n_row), in_specs=in_specs, out_specs=out_specs, out_shape=out_shape,
                          scratch_shapes=scratch, compiler_params=_params("parallel", "arbitrary"), name=name)(*args)


def _rmsnorm_kernel(x_ref, g_ref, o_ref):
    x = x_ref[...]
    y = x * lax.rsqrt(jnp.mean(x * x, axis=-1, keepdims=True) + NORM_EPS)
    o_ref[...] = (y * g_ref[...]).astype(o_ref.dtype)


def _rmsnorm(x, g, out_dtype, tm):
    rows, d = x.shape
    return pl.pallas_call(
        _rmsnorm_kernel,
        grid=(rows // tm,),
        in_specs=[pl.BlockSpec((tm, d), lambda i: (i, 0)), pl.BlockSpec((1, d), lambda i: (0, 0))],
        out_specs=pl.BlockSpec((tm, d), lambda i: (i, 0)),
        out_shape=jax.ShapeDtypeStruct((rows, d), out_dtype),
        compiler_params=_params("parallel"),
        name="rmsnorm",
    )(x, g.reshape(1, d))


def _matmul_kernel(x_ref, w_ref, o_ref, *, gate):
    p = jnp.dot(x_ref[...], w_ref[...], preferred_element_type=F32)
    o_ref[...] = _sigmoid(p) if gate else p


def _matmul(x, w, tm, tn, gate=False):
    rows, k = x.shape
    n = w.shape[1]
    return pl.pallas_call(
        functools.partial(_matmul_kernel, gate=gate),
        grid=(rows // tm, n // tn),
        in_specs=[pl.BlockSpec((tm, k), lambda i, j: (i, 0)), pl.BlockSpec((k, tn), lambda i, j: (0, j))],
        out_specs=pl.BlockSpec((tm, tn), lambda i, j: (i, j)),
        out_shape=jax.ShapeDtypeStruct((rows, n), F32),
        compiler_params=_params("parallel", "parallel"),
        name="in_proj",
    )(x, w)


def _shift_proj_kernel(x_ref, w_ref, mu_ref, m_ref, o_ref, carry, *, tile, tiles_per_seq):
    i = pl.program_id(1)

    @pl.when(i % tiles_per_seq == 0)
    def _():
        carry[...] = m_ref[...]

    p = jnp.dot(x_ref[...], w_ref[...], preferred_element_type=F32)
    row = lax.broadcasted_iota(jnp.int32, p.shape, 0)
    prev = jnp.where(row == 0, carry[7:8, :], pltpu.roll(p, 1, 0))
    carry[...] = p[tile - 8:tile, :]
    o_ref[...] = p + (prev - p) * mu_ref[0:1, :]


def _shift_proj(x, w, col_blk, mu, p_before, batch, tm, tn, cast=None):
    rows, k = x.shape
    n = mu.shape[0]
    tiles_per_seq = rows // batch // tm
    m_blk = p_before.shape[0] // 8 - 1
    mu8 = jnp.zeros((8, n), F32).at[0].set(mu)
    out = _proj_call(
        functools.partial(_shift_proj_kernel, tile=tm, tiles_per_seq=tiles_per_seq), n // tn, rows // tm,
        [pl.BlockSpec((tm, k), lambda j, i: (i, 0)), pl.BlockSpec((k, tn), lambda j, i: (0, col_blk(j))),
         pl.BlockSpec((8, tn), lambda j, i: (0, j)), pl.BlockSpec((8, tn), lambda j, i: (m_blk, j))],
        [pl.BlockSpec((tm, tn), lambda j, i: (i, j))], [jax.ShapeDtypeStruct((rows, n), F32)],
        [pltpu.VMEM((8, tn), F32)], "shift_proj", (x, w, mu8, p_before), cast)
    return out[0] if cast is None else out


_K_MU, _K_W0, _K_A0, _K_KK, _K_KA = range(5)


def _key_proj_kernel(x_ref, w_ref, par_ref, m_ref, zl_ref, w2_ref, a2_ref, k_ref, kk_ref, ka_ref, lw_ref, carry,
                     *, tile, tiles_per_seq):
    i = pl.program_id(1)

    @pl.when(i % tiles_per_seq == 0)
    def _():
        carry[...] = m_ref[...]

    par = par_ref[...]
    prow = lambda r: par[r:r + 1, :]
    zw = jnp.tanh(zl_ref[:, :LORA]).astype(BF16)
    za = zl_ref[:, LORA:].astype(BF16)
    xw = jnp.dot(zw, w2_ref[...], preferred_element_type=F32)
    xa = jnp.dot(za, a2_ref[...], preferred_element_type=F32)
    p = jnp.dot(x_ref[...], w_ref[...], preferred_element_type=F32)
    lw_ref[...] = LOG2_DECAY_SCALE * _sigmoid(prow(_K_W0) + xw)
    a = _sigmoid(prow(_K_A0) + xa)
    row = lax.broadcasted_iota(jnp.int32, p.shape, 0)
    prev = jnp.where(row == 0, carry[7:8, :], pltpu.roll(p, 1, 0))
    carry[...] = p[tile - 8:tile, :]
    z = p + (prev - p) * prow(_K_MU)
    k_ref[...] = z * (1.0 + (a - 1.0) * prow(_K_KA))
    kk = z * prow(_K_KK)
    sq = (kk * kk).astype(BF16)
    ones_bd = _ones_bd()
    ss = jnp.concatenate([jnp.dot(sq[:, c:c + GL], ones_bd, preferred_element_type=F32)
                          for c in range(0, sq.shape[1], GL)], axis=1)
    kk = kk * lax.rsqrt(jnp.maximum(ss, 1e-24))
    kk_ref[...] = kk
    ka_ref[...] = kk * a


def _key_proj(x, w, col_blk, par, p_before, z_lora, w2, a2, batch, tm, tn, cast=None):
    rows, k = x.shape
    n = par.shape[1]
    tiles_per_seq = rows // batch // tm
    m_blk = p_before.shape[0] // 8 - 1
    return _proj_call(
        functools.partial(_key_proj_kernel, tile=tm, tiles_per_seq=tiles_per_seq), n // tn, rows // tm,
        [pl.BlockSpec((tm, k), lambda j, i: (i, 0)), pl.BlockSpec((k, tn), lambda j, i: (0, col_blk(j))),
         pl.BlockSpec((8, tn), lambda j, i: (0, j)), pl.BlockSpec((8, tn), lambda j, i: (m_blk, j)),
         pl.BlockSpec((tm, 2 * LORA), lambda j, i: (i, 0)),
         pl.BlockSpec((LORA, tn), lambda j, i: (0, j)), pl.BlockSpec((LORA, tn), lambda j, i: (0, j))],
        [pl.BlockSpec((tm, tn), lambda j, i: (i, j))] * 4, [jax.ShapeDtypeStruct((rows, n), F32)] * 4,
        [pltpu.VMEM((8, tn), F32)], "key_proj", (x, w, par, p_before, z_lora, w2, a2), cast)


def _out_proj_kernel(a_ref, b_ref, wa_ref, wb_ref, h_ref, o_ref):
    acc = jnp.dot(a_ref[...], wa_ref[...], preferred_element_type=F32)
    acc += jnp.dot(b_ref[...], wb_ref[...], preferred_element_type=F32)
    o_ref[...] = h_ref[...] + acc


def _out_proj(ya, yb, w, h, tm, tn):
    rows, k = ya.shape
    n = w.shape[1]
    return pl.pallas_call(
        _out_proj_kernel,
        grid=(rows // tm, n // tn),
        in_specs=[pl.BlockSpec((tm, k), lambda i, j: (i, 0)), pl.BlockSpec((tm, k), lambda i, j: (i, 0)),
                  pl.BlockSpec((k, tn), lambda i, j: (0, j)), pl.BlockSpec((k, tn), lambda i, j: (1, j)),
                  pl.BlockSpec((tm, tn), lambda i, j: (i, j))],
        out_specs=pl.BlockSpec((tm, tn), lambda i, j: (i, j)),
        out_shape=jax.ShapeDtypeStruct((rows, n), F32),
        compiler_params=_params("parallel", "parallel"),
        name="out_proj",
    )(ya, yb, w, w, h)


def _mlp_up_kernel(x_ref, w_ref, o_ref):
    hid = jnp.dot(x_ref[...], w_ref[...], preferred_element_type=F32)
    o_ref[...] = jnp.square(jnp.maximum(hid, 0.0)).astype(o_ref.dtype)


def _mlp_up(x, w, tm, tn):
    rows, k = x.shape
    n = w.shape[1]
    return pl.pallas_call(
        _mlp_up_kernel,
        grid=(rows // tm, n // tn),
        in_specs=[pl.BlockSpec((tm, k), lambda i, j: (i, 0)), pl.BlockSpec((k, tn), lambda i, j: (0, j))],
        out_specs=pl.BlockSpec((tm, tn), lambda i, j: (i, j)),
        out_shape=jax.ShapeDtypeStruct((rows, n), BF16),
        compiler_params=_params("parallel", "parallel"),
        name="mlp_up",
    )(x, w)


def _mlp_down_kernel(x_ref, w_ref, h_ref, o_ref):
    @pl.when(pl.program_id(2) == 0)
    def _():
        o_ref[...] = h_ref[...]

    o_ref[...] += jnp.dot(x_ref[...], w_ref[...], preferred_element_type=F32)


def _mlp_down(x, w, h, tm, tn, tk):
    rows, k = x.shape
    n = w.shape[1]
    return pl.pallas_call(
        _mlp_down_kernel,
        grid=(rows // tm, n // tn, k // tk),
        in_specs=[pl.BlockSpec((tm, tk), lambda i, j, q: (i, q)), pl.BlockSpec((tk, tn), lambda i, j, q: (q, j)),
                  pl.BlockSpec((tm, tn), lambda i, j, q: (i, j))],
        out_specs=pl.BlockSpec((tm, tn), lambda i, j, q: (i, j)),
        out_shape=jax.ShapeDtypeStruct((rows, n), F32),
        compiler_params=_params("parallel", "parallel", "arbitrary"),
        name="mlp_down",
    )(x, w, h)


_P_RK, _P_LNW, _P_LNB = range(3)
_P_ROWS = 8


def _rwkv_kernel(r_ref, k_ref, v_ref, kk_ref, ka_ref, lw_ref, gate_ref, par_ref, s0_ref,
                 y_ref, sout_ref, state, y_s, *, tile, ng):
    t_idx = pl.program_id(2)

    @pl.when(t_idx == 0)
    def _():
        state[...] = s0_ref[...]

    par = par_ref[0]
    prow = lambda i: par[i:i + 1, :]
    ones_bd = _ones_bd()

    def head_sum(xb):
        return jnp.concatenate([jnp.dot(xb[:, g * GL:(g + 1) * GL], ones_bd, preferred_element_type=F32)
                                for g in range(ng)], axis=1)

    ti = lax.broadcasted_iota(jnp.int32, (CHUNK, GL), 0)
    lane = lax.broadcasted_iota(jnp.int32, (CHUNK, GL), 1)
    si = lane % HEAD
    strict = si < ti
    incl = si <= ti
    eye = jnp.where(si == ti, 1.0, 0.0).astype(F32)
    head_of_lane = [lane // HEAD == h for h in range(GROUP)]
    bd_mask = (lax.broadcasted_iota(jnp.int32, (GL, GL), 0) // HEAD
               == lax.broadcasted_iota(jnp.int32, (GL, GL), 1) // HEAD)
    tri2 = (lax.broadcasted_iota(jnp.int32, (CHUNK, 2 * CHUNK), 1) % CHUNK
            <= lax.broadcasted_iota(jnp.int32, (CHUNK, 2 * CHUNK), 0)).astype(BF16)

    def bd(x):
        xb = x.astype(BF16)
        zero = jnp.zeros((), BF16)
        return jnp.concatenate([jnp.where(m, xb, zero) for m in head_of_lane], axis=0)

    def rob_mm(x, y):
        return jnp.dot(x.astype(BF16), bd(y), preferred_element_type=F32)

    def nt_dot(x, y):
        return lax.dot_general(x, y, (((1,), (1,)), ((), ())), preferred_element_type=F32)

    def group_chunk(rows, g):
        cols = slice(g * GL, (g + 1) * GL)
        rc, kc, vc, kkc, kka, lwc = (s[rows, cols] for s in (r_ref, k_ref, v_ref, kk_ref, ka_ref, lw_ref))
        hi = lwc.astype(BF16)
        lo = (lwc - hi.astype(F32)).astype(BF16)
        cum = jnp.dot(tri2, jnp.concatenate([hi, lo], axis=0), preferred_element_type=F32)
        yield
        cum_end = cum[CHUNK - 1:CHUNK, :]
        e_cum = jnp.exp2(cum)
        e_neg = jnp.exp2(-cum)
        e_prev = jnp.exp2(cum - lwc)
        e_end = jnp.exp2(cum_end - cum)
        al = -kkc * e_prev
        rt = rc * e_cum
        bt = kka * e_neg
        kt = kc * e_neg
        bh = kka * e_end
        kh = kc * e_end
        lr = jnp.concatenate([al, rt], axis=0).astype(BF16)
        x_b = nt_dot(lr, bd(bt))
        x_k = nt_dot(lr, bd(kt))
        s_old = state[g]
        ars = nt_dot(lr, s_old.astype(BF16))
        yield
        a_ab = jnp.where(strict, x_b[:CHUNK], 0.0)
        a_rb = jnp.where(incl, x_b[CHUNK:], 0.0)
        a_ak = jnp.where(strict, x_k[:CHUNK], 0.0)
        a_rk = jnp.where(incl, x_k[CHUNK:], 0.0)
        av = ars + rob_mm(jnp.concatenate([a_ak, a_rk], axis=0), vc)
        steps = int(math.log2(CHUNK))
        pw = rob_mm(a_ab, a_ab)
        tinv = eye + a_ab
        yield
        for _ in range(steps - 2):
            both = rob_mm(jnp.concatenate([pw, tinv], axis=0), pw)
            pw = both[:CHUNK]
            tinv = tinv + both[CHUNK:]
            yield
        tinv = tinv + rob_mm(tinv, pw)
        yield
        u = rob_mm(tinv, av[:CHUNK])
        yield
        y_s[rows, cols] = av[CHUNK:] + rob_mm(a_rb, u)
        uv = jnp.concatenate([u, vc], axis=0).astype(BF16)
        bk = jnp.concatenate([bh, kh], axis=0).astype(BF16)
        ds = lax.dot_general(uv, bk, (((0,), (0,)), ((), ())), preferred_element_type=F32)
        state[g] = s_old * jnp.exp2(cum_end) + jnp.where(bd_mask, ds, 0.0)

    def chunk_body(c, _):
        rows = pl.ds(pl.multiple_of(c * CHUNK, CHUNK), CHUNK)
        live = [group_chunk(rows, g) for g in range(ng)]
        while live:
            live = [gen for gen in live if next(gen, StopIteration) is not StopIteration]
        return 0

    lax.fori_loop(0, tile // CHUNK, chunk_body, 0)

    y = y_s[...]
    inv_n = 1.0 / HEAD
    y_hi = y.astype(BF16)
    y_lo = (y - y_hi.astype(F32)).astype(BF16)
    d = y - (head_sum(y_hi) + head_sum(y_lo)) * inv_n
    var = head_sum((d * d).astype(BF16)) * inv_n
    yn = d * lax.rsqrt(var + GN_EPS) * prow(_P_LNW) + prow(_P_LNB)
    bonus = head_sum((r_ref[...] * k_ref[...] * prow(_P_RK)).astype(BF16)) * v_ref[...]
    y_ref[...] = (gate_ref[...] * (yn + bonus)).astype(y_ref.dtype)

    @pl.when(t_idx == pl.num_programs(2) - 1)
    def _():
        sout_ref[0] = state[...]


def _rwkv(z_rv, keyed, gate, par, s0, batch, tile, ng):
    rows = z_rv.shape[0]
    d = z_rv.shape[1] // 2
    groups = d // GL
    sg = groups // ng
    width = ng * GL
    nt = rows // batch // tile
    wide = lambda arr: arr.reshape(sg, ng, arr.shape[1], GL).transpose(0, 2, 1, 3).reshape(sg, arr.shape[1], width)
    row_blk = lambda off: pl.BlockSpec((tile, width), lambda b, g, t: (b * nt + t, off + g))
    k, kk, ka, lw = keyed
    return pl.pallas_call(
        functools.partial(_rwkv_kernel, tile=tile, ng=ng),
        grid=(batch, sg, nt),
        in_specs=[row_blk(0), row_blk(0), row_blk(sg), row_blk(0), row_blk(0), row_blk(0), row_blk(0),
                  pl.BlockSpec((1, _P_ROWS, width), lambda b, g, t: (g, 0, 0)),
                  pl.BlockSpec((ng, GL, GL), lambda b, g, t: (g, 0, 0))],
        out_specs=[row_blk(0), pl.BlockSpec((1, ng, GL, GL), lambda b, g, t: (b * sg + g, 0, 0, 0))],
        out_shape=[jax.ShapeDtypeStruct((rows, d), BF16), jax.ShapeDtypeStruct((batch * sg, ng, GL, GL), F32)],
        scratch_shapes=[pltpu.VMEM((ng, GL, GL), F32), pltpu.VMEM((tile, width), F32)],
        compiler_params=_params("parallel", "parallel", "arbitrary"),
        name="rwkv7",
    )(z_rv, k, z_rv, kk, ka, lw, gate, wide(par), s0)


def _conv_proj_kernel(u_ref, wb_ref, wc_ref, wh_ref, wg_ref, cw_ref, mc_ref, mh_ref, o_ref, carry,
                      *, tile, tiles_per_seq):
    i = pl.program_id(1)

    @pl.when(i % tiles_per_seq == 0)
    def _():
        carry[...] = mc_ref[...] * mh_ref[...]

    x = u_ref[...]
    proj = lambda w_ref: jnp.dot(x, w_ref[...], preferred_element_type=F32)
    u = proj(wc_ref) * proj(wh_ref)
    row = lax.broadcasted_iota(jnp.int32, u.shape, 0)
    prev1 = jnp.where(row == 0, carry[7:8, :], pltpu.roll(u, 1, 0))
    prev2 = jnp.where(row == 0, carry[6:7, :], jnp.where(row == 1, carry[7:8, :], pltpu.roll(u, 2, 0)))
    carry[...] = u[tile - 8:tile, :]
    w = cw_ref[...]
    conv = w[0:1, :] * prev2 + w[1:2, :] * prev1 + w[2:3, :] * u
    o_ref[...] = (_sigmoid(proj(wg_ref)) * (proj(wb_ref) * conv)).astype(o_ref.dtype)


def _conv_proj(u, w, col0, gate_col0, conv_w, p_conv_meta, batch, tile, tc, cast=None):
    rows, k = u.shape
    d = conv_w.shape[1]
    nc = d // tc
    tiles_per_seq = rows // batch // tile
    meta_blk = p_conv_meta.shape[0] // 8 - 1
    wblk = lambda col: pl.BlockSpec((k, tc), lambda j, i: (0, col // tc + j))
    w8 = jnp.zeros((8, d), F32).at[:conv_w.shape[0]].set(conv_w)
    out = _proj_call(
        functools.partial(_conv_proj_kernel, tile=tile, tiles_per_seq=tiles_per_seq), nc, rows // tile,
        [pl.BlockSpec((tile, k), lambda j, i: (i, 0)),
         wblk(col0), wblk(col0 + d), wblk(col0 + 2 * d), wblk(gate_col0),
         pl.BlockSpec((8, tc), lambda j, i: (0, j)),
         pl.BlockSpec((8, tc), lambda j, i: (meta_blk, nc + j)),
         pl.BlockSpec((8, tc), lambda j, i: (meta_blk, 2 * nc + j))],
        [pl.BlockSpec((tile, tc), lambda j, i: (i, j))], [jax.ShapeDtypeStruct((rows, d), BF16)],
        [pltpu.VMEM((8, tc), F32)], "conv_proj", (u, w, w, w, w, w8, p_conv_meta, p_conv_meta), cast)
    return out[0] if cast is None else out


def _pick(n, prefs):
    for p in prefs:
        if n % p == 0:
            return p
    raise ValueError(f"no tile for {n} in {prefs}")


def kernel(x, meta_tokens, norm_mix_g, w_in, rwkv_shift_mu, rwkv_w0, rwkv_w2, rwkv_a0, rwkv_a2, rwkv_k_k, rwkv_k_a, rwkv_r_k, rwkv_ln_w, rwkv_ln_b, conv_w, w_out, norm_mlp_g, w_up, w_down, norm_final_g):
    bsz, seq, d = x.shape
    n_meta = meta_tokens.shape[0]
    depth = w_in.shape[0]
    assert depth == 1, "the meta-token hand-off is written for a single layer"
    assert d % GL == 0 and seq % CHUNK == 0 and n_meta <= META_ROWS
    groups = d // GL
    rows = bsz * seq
    layer = 0

    c_rkv, c_lora = 3 * d, 2 * LORA
    c0 = c_rkv + c_lora
    c_ga = c0 + 3 * d
    tc = _pick(d, (256, 128))
    assert c_rkv % c_lora == 0 and c0 % tc == 0, "column segments must start on block boundaries"
    w_all = w_in[layer].astype(BF16)
    w_ga = w_all[:, c_ga:c_ga + d]
    w2 = rwkv_w2[layer].astype(BF16)
    a2 = rwkv_a2[layer].astype(BF16)

    mu = rwkv_shift_mu[layer]
    mu_rv = jnp.concatenate([mu[:d], mu[2 * d:c_rkv]])
    rows8 = lambda vecs: jnp.zeros((8, d), F32).at[:len(vecs)].set(jnp.stack(vecs))
    key_par = rows8([mu[d:2 * d], rwkv_w0[layer], rwkv_a0[layer], rwkv_k_k[layer], rwkv_k_a[layer]])
    per_group = lambda vec: vec.reshape(groups, 1, GL)
    par = jnp.concatenate(
        [per_group(rwkv_r_k[layer].reshape(d)), per_group(rwkv_ln_w[layer]), per_group(rwkv_ln_b[layer]),
         jnp.zeros((groups, _P_ROWS - 3, GL), F32)], axis=1)
    tw = _pick(d, (1024, 512, 256))
    tk = _pick(d, (512, 256))
    rv_blk = lambda j: j + (j // (d // tw)) * (d // tw)
    k_blk = lambda j: d // tk + j
    lora_blk = lambda j: c_rkv // c_lora + j

    def mixer_inputs(h_rows, batch, tm, before, casts=(None, None)):
        u = _rmsnorm(h_rows, norm_mix_g[layer], BF16, min(tm, 256))
        z_lora = _shift_proj(u, w_all, lora_blk, mu[c_rkv:c0], before[0], batch, tm, c_lora)
        z_rv = _shift_proj(u, w_all, rv_blk, mu_rv, before[1], batch, tm, tw, casts[0])
        keyed = _key_proj(u, w_all, k_blk, key_par, before[2], z_lora, w2, a2, batch, tm, tk, casts[1])
        return u, z_rv, keyed, _matmul(u, w_ga, tm, _pick(d, (1024, 512, 256)), gate=True)

    meta = jnp.zeros((META_ROWS, d), F32).at[META_ROWS - n_meta:].set(meta_tokens.astype(F32))
    nothing = (jnp.zeros((8, c_lora), F32), jnp.zeros((8, 2 * d), F32), jnp.zeros((8, d), F32))
    u_meta, mz_rv, m_keyed, m_gate = mixer_inputs(meta, 1, META_ROWS, nothing)

    def raw(col_blk, n, tn):
        return _shift_proj(u_meta, w_all, col_blk, jnp.zeros((n,), F32), jnp.zeros((8, n), F32), 1, META_ROWS, tn)

    ng = _pick(groups, (8, 4, 2, 1))
    _, s_meta = _rwkv(mz_rv, m_keyed, m_gate, par, jnp.zeros((groups, GL, GL), F32), 1, META_ROWS, ng)

    xf = x.reshape(rows, d)
    tm = _pick(rows, (1024, 512, 256, 128, 64))
    t_seq = _pick(seq, (1024, 512, 256, 128, 64))
    m_conv = raw(lambda j: c0 // tc + j, 3 * d, tc)
    n_row = rows // t_seq
    ride = lambda wgt, steps: wgt if _cast_rows(wgt, steps) else None
    casts = (ride(w_down[layer], 2 * d // tw * n_row), ride(w_out[layer], d // tk * n_row), ride(w_up[layer], d // tc * n_row))
    u, z_rv, keyed, gate = mixer_inputs(xf, bsz, t_seq, (raw(lora_blk, c_lora, c_lora), raw(rv_blk, 2 * d, tw),
                                                         raw(k_blk, d, tk)), casts[:2])
    z_rv, wd = z_rv if casts[0] is not None else (z_rv, w_down[layer].astype(BF16))
    keyed, wo = (keyed[:4], keyed[4]) if casts[1] is not None else (keyed, w_out[layer].astype(BF16))
    t_scan = _pick(seq, (256, 128, 64))
    ya, _ = _rwkv(z_rv, keyed, gate, par, s_meta.reshape(groups, GL, GL), bsz, t_scan, ng)
    yb = _conv_proj(u, w_all, c0, c_ga + d, conv_w[layer], m_conv, bsz, t_seq, tc, casts[2])
    yb, wu = yb if casts[2] is not None else (yb, w_up[layer].astype(BF16))
    tq = _pick(rows, (512, 256, 128, 64))
    h1 = _out_proj(ya, yb, wo, xf, tq, _pick(d, (512, 256, 128)))
    u2 = _rmsnorm(h1, norm_mlp_g[layer], BF16, min(tq, 256))
    big = (1024, 512, 256, 128)
    hid = _mlp_up(u2, wu, tm, _pick(wu.shape[1], big))
    h2 = _mlp_down(hid, wd, h1, tm, _pick(d, big), _pick(wu.shape[1], (4096, 2048, 1024, 512)))
    out = _rmsnorm(h2, norm_final_g, F32, min(tq, 256))
    return out.reshape(bsz, seq, d)
```

```python
import functools
import math

import jax
import jax.numpy as jnp
from jax import lax
from jax.experimental import pallas as pl
from jax.experimental.pallas import tpu as pltpu

F32 = jnp.float32
BF16 = jnp.bfloat16

HEAD = 64
GROUP = 4
GL = GROUP * HEAD
CHUNK = 64
LORA = 128
NORM_EPS = 1e-6
GN_EPS = 64e-5
LOG2_DECAY_SCALE = -math.exp(-0.5) / math.log(2.0)
META_ROWS = 64
VMEM_LIMIT = 56 * 1024 * 1024
ROW_TILES = (1024, 512, 256, 128, 64)
COL_TILES = (1024, 512, 256, 128)


def _params(*sem):
    return pltpu.CompilerParams(dimension_semantics=sem, vmem_limit_bytes=VMEM_LIMIT)


def _sigmoid(x):
    return 0.5 * jnp.tanh(0.5 * x) + 0.5


def _ones_bd():
    return (lax.broadcasted_iota(jnp.int32, (GL, GL), 0) // HEAD
            == lax.broadcasted_iota(jnp.int32, (GL, GL), 1) // HEAD).astype(BF16)


def _with_cast(body, n_in, n_out):
    def kernel(*refs):
        src, dst = refs[n_in], refs[n_in + 1 + n_out]
        dst[...] = src[...].astype(dst.dtype)
        body(*refs[:n_in], *refs[n_in + 1:n_in + 1 + n_out], *refs[n_in + n_out + 2:])

    return kernel


def _cast_rows(src, steps):
    chunk = src.shape[0] // steps
    return chunk if src.shape[0] % steps == 0 and chunk % 16 == 0 else None


def _proj_call(body, n_col, n_row, in_specs, out_specs, out_shape, scratch, name, args, cast):
    if cast is not None:
        spec = pl.BlockSpec((_cast_rows(cast, n_col * n_row), cast.shape[1]), lambda j, i: (j * n_row + i, 0))
        body = _with_cast(body, len(in_specs), len(out_specs))
        in_specs, out_specs = [*in_specs, spec], [*out_specs, spec]
        out_shape, args = [*out_shape, jax.ShapeDtypeStruct(cast.shape, BF16)], (*args, cast)
    return pl.pallas_call(body, grid=(n_col, n_row), in_specs=in_specs, out_specs=out_specs, out_shape=out_shape,
                          scratch_shapes=scratch, compiler_params=_params("parallel", "arbitrary"), name=name)(*args)


def _rmsnorm_kernel(x_ref, g_ref, o_ref):
    x = x_ref[...]
    y = x * lax.rsqrt(jnp.mean(x * x, axis=-1, keepdims=True) + NORM_EPS)
    o_ref[...] = (y * g_ref[...]).astype(o_ref.dtype)


def _rmsnorm(x, g, out_dtype, tm):
    rows, d = x.shape
    return pl.pallas_call(
        _rmsnorm_kernel,
        grid=(rows // tm,),
        in_specs=[pl.BlockSpec((tm, d), lambda i: (i, 0)), pl.BlockSpec((1, d), lambda i: (0, 0))],
        out_specs=pl.BlockSpec((tm, d), lambda i: (i, 0)),
        out_shape=jax.ShapeDtypeStruct((rows, d), out_dtype),
        compiler_params=_params("parallel"),
        name="rmsnorm",
    )(x, g.reshape(1, d))


def _matmul_kernel(x_ref, w_ref, o_ref, *, gate):
    p = jnp.dot(x_ref[...], w_ref[...], preferred_element_type=F32)
    o_ref[...] = _sigmoid(p) if gate else p


def _matmul(x, w, tm, tn, gate=False):
    rows, k = x.shape
    n = w.shape[1]
    return pl.pallas_call(
        functools.partial(_matmul_kernel, gate=gate),
        grid=(rows // tm, n // tn),
        in_specs=[pl.BlockSpec((tm, k), lambda i, j: (i, 0)), pl.BlockSpec((k, tn), lambda i, j: (0, j))],
        out_specs=pl.BlockSpec((tm, tn), lambda i, j: (i, j)),
        out_shape=jax.ShapeDtypeStruct((rows, n), F32),
        compiler_params=_params("parallel", "parallel"),
        name="in_proj",
    )(x, w)


def _shift_proj_kernel(x_ref, w_ref, mu_ref, m_ref, o_ref, *rest, tile, tiles_per_seq):
    i = pl.program_id(1)
    carry = rest[-1]

    @pl.when(i % tiles_per_seq == 0)
    def _():
        carry[...] = m_ref[...]

    p = jnp.dot(x_ref[...], w_ref[...], preferred_element_type=F32)
    row = lax.broadcasted_iota(jnp.int32, p.shape, 0)
    prev = jnp.where(row == 0, carry[7:8, :], pltpu.roll(p, 1, 0))
    carry[...] = p[tile - 8:tile, :]
    o_ref[...] = p + (prev - p) * mu_ref[0:1, :]
    for raw_ref in rest[:-1]:
        raw_ref[...] = p


def _shift_proj(x, w, col_blk, mu, p_before, batch, tm, tn, cast=None, raw=False):
    rows, k = x.shape
    n = mu.shape[0]
    tiles_per_seq = rows // batch // tm
    m_blk = p_before.shape[0] // 8 - 1
    mu8 = jnp.zeros((8, n), F32).at[0].set(mu)
    out = _proj_call(
        functools.partial(_shift_proj_kernel, tile=tm, tiles_per_seq=tiles_per_seq), n // tn, rows // tm,
        [pl.BlockSpec((tm, k), lambda j, i: (i, 0)), pl.BlockSpec((k, tn), lambda j, i: (0, col_blk(j))),
         pl.BlockSpec((8, tn), lambda j, i: (0, j)), pl.BlockSpec((8, tn), lambda j, i: (m_blk, j))],
        [pl.BlockSpec((tm, tn), lambda j, i: (i, j))] * (1 + raw), [jax.ShapeDtypeStruct((rows, n), F32)] * (1 + raw),
        [pltpu.VMEM((8, tn), F32)], "shift_proj", (x, w, mu8, p_before), cast)
    return out[0] if len(out) == 1 else out


_K_MU, _K_W0, _K_A0, _K_KK, _K_KA = range(5)


def _key_proj_kernel(x_ref, w_ref, par_ref, m_ref, zl_ref, w2_ref, a2_ref, k_ref, kk_ref, ka_ref, lw_ref, *rest,
                     tile, tiles_per_seq):
    i = pl.program_id(1)
    carry = rest[-1]

    @pl.when(i % tiles_per_seq == 0)
    def _():
        carry[...] = m_ref[...]

    par = par_ref[...]
    prow = lambda r: par[r:r + 1, :]
    zw = jnp.tanh(zl_ref[:, :LORA]).astype(BF16)
    za = zl_ref[:, LORA:].astype(BF16)
    xw = jnp.dot(zw, w2_ref[...], preferred_element_type=F32)
    xa = jnp.dot(za, a2_ref[...], preferred_element_type=F32)
    p = jnp.dot(x_ref[...], w_ref[...], preferred_element_type=F32)
    lw_ref[...] = LOG2_DECAY_SCALE * _sigmoid(prow(_K_W0) + xw)
    a = _sigmoid(prow(_K_A0) + xa)
    row = lax.broadcasted_iota(jnp.int32, p.shape, 0)
    prev = jnp.where(row == 0, carry[7:8, :], pltpu.roll(p, 1, 0))
    carry[...] = p[tile - 8:tile, :]
    for raw_ref in rest[:-1]:
        raw_ref[...] = p
    z = p + (prev - p) * prow(_K_MU)
    k_ref[...] = z * (1.0 + (a - 1.0) * prow(_K_KA))
    kk = z * prow(_K_KK)
    sq = (kk * kk).astype(BF16)
    ones_bd = _ones_bd()
    ss = jnp.concatenate([jnp.dot(sq[:, c:c + GL], ones_bd, preferred_element_type=F32)
                          for c in range(0, sq.shape[1], GL)], axis=1)
    kk = kk * lax.rsqrt(jnp.maximum(ss, 1e-24))
    kk_ref[...] = kk
    ka_ref[...] = kk * a


def _key_proj(x, w, col_blk, par, p_before, z_lora, w2, a2, batch, tm, tn, cast=None, raw=False):
    rows, k = x.shape
    n = par.shape[1]
    tiles_per_seq = rows // batch // tm
    m_blk = p_before.shape[0] // 8 - 1
    return _proj_call(
        functools.partial(_key_proj_kernel, tile=tm, tiles_per_seq=tiles_per_seq), n // tn, rows // tm,
        [pl.BlockSpec((tm, k), lambda j, i: (i, 0)), pl.BlockSpec((k, tn), lambda j, i: (0, col_blk(j))),
         pl.BlockSpec((8, tn), lambda j, i: (0, j)), pl.BlockSpec((8, tn), lambda j, i: (m_blk, j)),
         pl.BlockSpec((tm, 2 * LORA), lambda j, i: (i, 0)),
         pl.BlockSpec((LORA, tn), lambda j, i: (0, j)), pl.BlockSpec((LORA, tn), lambda j, i: (0, j))],
        [pl.BlockSpec((tm, tn), lambda j, i: (i, j))] * (4 + raw), [jax.ShapeDtypeStruct((rows, n), F32)] * (4 + raw),
        [pltpu.VMEM((8, tn), F32)], "key_proj", (x, w, par, p_before, z_lora, w2, a2), cast)


def _out_proj_kernel(a_ref, b_ref, wa_ref, wb_ref, h_ref, o_ref):
    acc = jnp.dot(a_ref[...], wa_ref[...], preferred_element_type=F32)
    acc += jnp.dot(b_ref[...], wb_ref[...], preferred_element_type=F32)
    o_ref[...] = h_ref[...] + acc


def _out_proj(ya, yb, w, h, tm, tn):
    rows, k = ya.shape
    n = w.shape[1]
    return pl.pallas_call(
        _out_proj_kernel,
        grid=(rows // tm, n // tn),
        in_specs=[pl.BlockSpec((tm, k), lambda i, j: (i, 0)), pl.BlockSpec((tm, k), lambda i, j: (i, 0)),
                  pl.BlockSpec((k, tn), lambda i, j: (0, j)), pl.BlockSpec((k, tn), lambda i, j: (1, j)),
                  pl.BlockSpec((tm, tn), lambda i, j: (i, j))],
        out_specs=pl.BlockSpec((tm, tn), lambda i, j: (i, j)),
        out_shape=jax.ShapeDtypeStruct((rows, n), F32),
        compiler_params=_params("parallel", "parallel"),
        name="out_proj",
    )(ya, yb, w, w, h)


def _mlp_up_kernel(x_ref, w_ref, o_ref):
    hid = jnp.dot(x_ref[...], w_ref[...], preferred_element_type=F32)
    o_ref[...] = jnp.square(jnp.maximum(hid, 0.0)).astype(o_ref.dtype)


def _mlp_up(x, w, tm, tn):
    rows, k = x.shape
    n = w.shape[1]
    return pl.pallas_call(
        _mlp_up_kernel,
        grid=(rows // tm, n // tn),
        in_specs=[pl.BlockSpec((tm, k), lambda i, j: (i, 0)), pl.BlockSpec((k, tn), lambda i, j: (0, j))],
        out_specs=pl.BlockSpec((tm, tn), lambda i, j: (i, j)),
        out_shape=jax.ShapeDtypeStruct((rows, n), BF16),
        compiler_params=_params("parallel", "parallel"),
        name="mlp_up",
    )(x, w)


def _mlp_down_kernel(x_ref, w_ref, h_ref, o_ref):
    @pl.when(pl.program_id(2) == 0)
    def _():
        o_ref[...] = h_ref[...]

    o_ref[...] += jnp.dot(x_ref[...], w_ref[...], preferred_element_type=F32)


def _mlp_down(x, w, h, tm, tn, tk):
    rows, k = x.shape
    n = w.shape[1]
    return pl.pallas_call(
        _mlp_down_kernel,
        grid=(rows // tm, n // tn, k // tk),
        in_specs=[pl.BlockSpec((tm, tk), lambda i, j, q: (i, q)), pl.BlockSpec((tk, tn), lambda i, j, q: (q, j)),
                  pl.BlockSpec((tm, tn), lambda i, j, q: (i, j))],
        out_specs=pl.BlockSpec((tm, tn), lambda i, j, q: (i, j)),
        out_shape=jax.ShapeDtypeStruct((rows, n), F32),
        compiler_params=_params("parallel", "parallel", "arbitrary"),
        name="mlp_down",
    )(x, w, h)


_P_RK, _P_LNW, _P_LNB = range(3)
_P_ROWS = 8


def _rwkv_kernel(r_ref, k_ref, v_ref, kk_ref, ka_ref, lw_ref, gate_ref, par_ref, s0_ref,
                 y_ref, sout_ref, state, y_s, *, tile, ng):
    t_idx = pl.program_id(2)

    @pl.when(t_idx == 0)
    def _():
        state[...] = s0_ref[...]

    par = par_ref[0]
    prow = lambda i: par[i:i + 1, :]
    ones_bd = _ones_bd()

    def head_sum(xb):
        return jnp.concatenate([jnp.dot(xb[:, g * GL:(g + 1) * GL], ones_bd, preferred_element_type=F32)
                                for g in range(ng)], axis=1)

    ti = lax.broadcasted_iota(jnp.int32, (CHUNK, GL), 0)
    lane = lax.broadcasted_iota(jnp.int32, (CHUNK, GL), 1)
    si = lane % HEAD
    strict = si < ti
    incl = si <= ti
    eye = jnp.where(si == ti, 1.0, 0.0).astype(F32)
    head_of_lane = [lane // HEAD == h for h in range(GROUP)]
    bd_mask = (lax.broadcasted_iota(jnp.int32, (GL, GL), 0) // HEAD
               == lax.broadcasted_iota(jnp.int32, (GL, GL), 1) // HEAD)
    tri2 = (lax.broadcasted_iota(jnp.int32, (CHUNK, 2 * CHUNK), 1) % CHUNK
            <= lax.broadcasted_iota(jnp.int32, (CHUNK, 2 * CHUNK), 0)).astype(BF16)

    def bd(x):
        xb = x.astype(BF16)
        zero = jnp.zeros((), BF16)
        return jnp.concatenate([jnp.where(m, xb, zero) for m in head_of_lane], axis=0)

    def rob_mm(x, y):
        return jnp.dot(x.astype(BF16), bd(y), preferred_element_type=F32)

    def nt_dot(x, y):
        return lax.dot_general(x, y, (((1,), (1,)), ((), ())), preferred_element_type=F32)

    def group_chunk(rows, g):
        cols = slice(g * GL, (g + 1) * GL)
        rc, kc, vc, kkc, kka, lwc = (s[rows, cols] for s in (r_ref, k_ref, v_ref, kk_ref, ka_ref, lw_ref))
        hi = lwc.astype(BF16)
        lo = (lwc - hi.astype(F32)).astype(BF16)
        cum = jnp.dot(tri2, jnp.concatenate([hi, lo], axis=0), preferred_element_type=F32)
        yield
        cum_end = cum[CHUNK - 1:CHUNK, :]
        e_cum = jnp.exp2(cum)
        e_neg = jnp.exp2(-cum)
        e_prev = jnp.exp2(cum - lwc)
        e_end = jnp.exp2(cum_end - cum)
        al = -kkc * e_prev
        rt = rc * e_cum
        bt = kka * e_neg
        kt = kc * e_neg
        bh = kka * e_end
        kh = kc * e_end
        lr = jnp.concatenate([al, rt], axis=0).astype(BF16)
        x_b = nt_dot(lr, bd(bt))
        x_k = nt_dot(lr, bd(kt))
        s_old = state[g]
        ars = nt_dot(lr, s_old.astype(BF16))
        yield
        a_ab = jnp.where(strict, x_b[:CHUNK], 0.0)
        a_rb = jnp.where(incl, x_b[CHUNK:], 0.0)
        a_ak = jnp.where(strict, x_k[:CHUNK], 0.0)
        a_rk = jnp.where(incl, x_k[CHUNK:], 0.0)
        av = ars + rob_mm(jnp.concatenate([a_ak, a_rk], axis=0), vc)
        steps = int(math.log2(CHUNK))
        pw = rob_mm(a_ab, a_ab)
        tinv = eye + a_ab
        yield
        for _ in range(steps - 2):
            both = rob_mm(jnp.concatenate([pw, tinv], axis=0), pw)
            pw = both[:CHUNK]
            tinv = tinv + both[CHUNK:]
            yield
        tinv = tinv + rob_mm(tinv, pw)
        yield
        u = rob_mm(tinv, av[:CHUNK])
        yield
        y_s[rows, cols] = av[CHUNK:] + rob_mm(a_rb, u)
        uv = jnp.concatenate([u, vc], axis=0).astype(BF16)
        bk = jnp.concatenate([bh, kh], axis=0).astype(BF16)
        ds = lax.dot_general(uv, bk, (((0,), (0,)), ((), ())), preferred_element_type=F32)
        state[g] = s_old * jnp.exp2(cum_end) + jnp.where(bd_mask, ds, 0.0)

    def chunk_body(c, _):
        rows = pl.ds(pl.multiple_of(c * CHUNK, CHUNK), CHUNK)
        live = [group_chunk(rows, g) for g in range(ng)]
        while live:
            live = [gen for gen in live if next(gen, StopIteration) is not StopIteration]
        return 0

    lax.fori_loop(0, tile // CHUNK, chunk_body, 0)

    y = y_s[...]
    inv_n = 1.0 / HEAD
    y_hi = y.astype(BF16)
    y_lo = (y - y_hi.astype(F32)).astype(BF16)
    d = y - (head_sum(y_hi) + head_sum(y_lo)) * inv_n
    var = head_sum((d * d).astype(BF16)) * inv_n
    yn = d * lax.rsqrt(var + GN_EPS) * prow(_P_LNW) + prow(_P_LNB)
    bonus = head_sum((r_ref[...] * k_ref[...] * prow(_P_RK)).astype(BF16)) * v_ref[...]
    y_ref[...] = (gate_ref[...] * (yn + bonus)).astype(y_ref.dtype)

    @pl.when(t_idx == pl.num_programs(2) - 1)
    def _():
        sout_ref[0] = state[...]


def _rwkv(z_rv, keyed, gate, par, s0, batch, tile, ng):
    rows = z_rv.shape[0]
    d = z_rv.shape[1] // 2
    groups = d // GL
    sg = groups // ng
    width = ng * GL
    nt = rows // batch // tile
    wide = lambda arr: arr.reshape(sg, ng, arr.shape[1], GL).transpose(0, 2, 1, 3).reshape(sg, arr.shape[1], width)
    row_blk = lambda off: pl.BlockSpec((tile, width), lambda b, g, t: (b * nt + t, off + g))
    k, kk, ka, lw = keyed
    return pl.pallas_call(
        functools.partial(_rwkv_kernel, tile=tile, ng=ng),
        grid=(batch, sg, nt),
        in_specs=[row_blk(0), row_blk(0), row_blk(sg), row_blk(0), row_blk(0), row_blk(0), row_blk(0),
                  pl.BlockSpec((1, _P_ROWS, width), lambda b, g, t: (g, 0, 0)),
                  pl.BlockSpec((ng, GL, GL), lambda b, g, t: (g, 0, 0))],
        out_specs=[row_blk(0), pl.BlockSpec((1, ng, GL, GL), lambda b, g, t: (b * sg + g, 0, 0, 0))],
        out_shape=[jax.ShapeDtypeStruct((rows, d), BF16), jax.ShapeDtypeStruct((batch * sg, ng, GL, GL), F32)],
        scratch_shapes=[pltpu.VMEM((ng, GL, GL), F32), pltpu.VMEM((tile, width), F32)],
        compiler_params=_params("parallel", "parallel", "arbitrary"),
        name="rwkv7",
    )(z_rv, k, z_rv, kk, ka, lw, gate, wide(par), s0)


def _conv_proj_kernel(u_ref, wb_ref, wc_ref, wh_ref, wg_ref, cw_ref, mc_ref, mh_ref, o_ref, carry,
                      *, tile, tiles_per_seq):
    i = pl.program_id(1)

    @pl.when(i % tiles_per_seq == 0)
    def _():
        carry[...] = mc_ref[...] * mh_ref[...]

    x = u_ref[...]
    proj = lambda w_ref: jnp.dot(x, w_ref[...], preferred_element_type=F32)
    u = proj(wc_ref) * proj(wh_ref)
    row = lax.broadcasted_iota(jnp.int32, u.shape, 0)
    prev1 = jnp.where(row == 0, carry[7:8, :], pltpu.roll(u, 1, 0))
    prev2 = jnp.where(row == 0, carry[6:7, :], jnp.where(row == 1, carry[7:8, :], pltpu.roll(u, 2, 0)))
    carry[...] = u[tile - 8:tile, :]
    w = cw_ref[...]
    conv = w[0:1, :] * prev2 + w[1:2, :] * prev1 + w[2:3, :] * u
    o_ref[...] = (_sigmoid(proj(wg_ref)) * (proj(wb_ref) * conv)).astype(o_ref.dtype)


def _conv_proj(u, w, col0, gate_col0, conv_w, p_conv_meta, batch, tile, tc, cast=None):
    rows, k = u.shape
    d = conv_w.shape[1]
    nc = d // tc
    tiles_per_seq = rows // batch // tile
    meta_blk = p_conv_meta.shape[0] // 8 - 1
    wblk = lambda col: pl.BlockSpec((k, tc), lambda j, i: (0, col // tc + j))
    w8 = jnp.zeros((8, d), F32).at[:conv_w.shape[0]].set(conv_w)
    out = _proj_call(
        functools.partial(_conv_proj_kernel, tile=tile, tiles_per_seq=tiles_per_seq), nc, rows // tile,
        [pl.BlockSpec((tile, k), lambda j, i: (i, 0)),
         wblk(col0), wblk(col0 + d), wblk(col0 + 2 * d), wblk(gate_col0),
         pl.BlockSpec((8, tc), lambda j, i: (0, j)),
         pl.BlockSpec((8, tc), lambda j, i: (meta_blk, nc + j)),
         pl.BlockSpec((8, tc), lambda j, i: (meta_blk, 2 * nc + j))],
        [pl.BlockSpec((tile, tc), lambda j, i: (i, j))], [jax.ShapeDtypeStruct((rows, d), BF16)],
        [pltpu.VMEM((8, tc), F32)], "conv_proj", (u, w, w, w, w, w8, p_conv_meta, p_conv_meta), cast)
    return out[0] if cast is None else out


def _pick(n, prefs):
    for p in prefs:
        if n % p == 0:
            return p
    raise ValueError(f"no tile for {n} in {prefs}")


def kernel(x, meta_tokens, norm_mix_g, w_in, rwkv_shift_mu, rwkv_w0, rwkv_w2, rwkv_a0, rwkv_a2, rwkv_k_k, rwkv_k_a, rwkv_r_k, rwkv_ln_w, rwkv_ln_b, conv_w, w_out, norm_mlp_g, w_up, w_down, norm_final_g):
    bsz, seq, d = x.shape
    n_meta = meta_tokens.shape[0]
    depth = w_in.shape[0]
    assert depth == 1, "the meta-token hand-off is written for a single layer"
    assert d % GL == 0 and seq % CHUNK == 0 and n_meta <= META_ROWS
    groups = d // GL
    rows = bsz * seq
    layer = 0

    c_rkv, c_lora = 3 * d, 2 * LORA
    c0 = c_rkv + c_lora
    c_ga = c0 + 3 * d
    tc = _pick(d, (256, 128))
    assert c_rkv % c_lora == 0 and c0 % tc == 0, "column segments must start on block boundaries"
    w_all = w_in[layer].astype(BF16)
    w_ga = w_all[:, c_ga:c_ga + d]
    w2 = rwkv_w2[layer].astype(BF16)
    a2 = rwkv_a2[layer].astype(BF16)

    mu = rwkv_shift_mu[layer]
    mu_rv = jnp.concatenate([mu[:d], mu[2 * d:c_rkv]])
    rows8 = lambda vecs: jnp.zeros((8, d), F32).at[:len(vecs)].set(jnp.stack(vecs))
    key_par = rows8([mu[d:2 * d], rwkv_w0[layer], rwkv_a0[layer], rwkv_k_k[layer], rwkv_k_a[layer]])
    per_group = lambda vec: vec.reshape(groups, 1, GL)
    par = jnp.concatenate(
        [per_group(rwkv_r_k[layer].reshape(d)), per_group(rwkv_ln_w[layer]), per_group(rwkv_ln_b[layer]),
         jnp.zeros((groups, _P_ROWS - 3, GL), F32)], axis=1)
    tw = _pick(d, COL_TILES)
    tk = _pick(d, (512, 256))
    rv_blk = lambda j: j + (j // (d // tw)) * (d // tw)
    k_blk = lambda j: d // tk + j
    lora_blk = lambda j: c_rkv // c_lora + j

    def mixer_inputs(h_rows, batch, tm, before, casts=(None, None), raw=False):
        u = _rmsnorm(h_rows, norm_mix_g[layer], BF16, min(tm, 256))
        z_lora = _shift_proj(u, w_all, lora_blk, mu[c_rkv:c0], before[0], batch, tm, c_lora, raw=raw)
        z_rv = _shift_proj(u, w_all, rv_blk, mu_rv, before[1], batch, tm, tw, casts[0], raw)
        keyed = _key_proj(u, w_all, k_blk, key_par, before[2], z_lora[0] if raw else z_lora, w2, a2, batch, tm, tk,
                          casts[1], raw)
        return u, z_lora, z_rv, keyed, _matmul(u, w_ga, tm, _pick(d, COL_TILES), gate=True)

    meta = jnp.zeros((META_ROWS, d), F32).at[META_ROWS - n_meta:].set(meta_tokens.astype(F32))
    nothing = (jnp.zeros((8, c_lora), F32), jnp.zeros((8, 2 * d), F32), jnp.zeros((8, d), F32))
    u_meta, (_, m_lora), (mz_rv, m_rv), m_keyed, m_gate = mixer_inputs(meta, 1, META_ROWS, nothing, raw=True)
    ng = _pick(groups, (8, 4, 2, 1))
    _, s_meta = _rwkv(mz_rv, m_keyed[:4], m_gate, par, jnp.zeros((groups, GL, GL), F32), 1, META_ROWS, ng)

    xf = x.reshape(rows, d)
    tm = _pick(rows, ROW_TILES)
    t_seq = _pick(seq, ROW_TILES)
    m_conv = _shift_proj(u_meta, w_all, lambda j: c0 // tc + j, jnp.zeros((3 * d,), F32), jnp.zeros((8, 3 * d), F32),
                         1, META_ROWS, tc)
    n_row = rows // t_seq
    ride = lambda wgt, steps: wgt if _cast_rows(wgt, steps) else None
    casts = (ride(w_down[layer], 2 * d // tw * n_row), ride(w_out[layer], d // tk * n_row), ride(w_up[layer], d // tc * n_row))
    u, _, z_rv, keyed, gate = mixer_inputs(xf, bsz, t_seq, (m_lora, m_rv, m_keyed[4]), casts[:2])
    z_rv, wd = z_rv if casts[0] is not None else (z_rv, w_down[layer].astype(BF16))
    keyed, wo = (keyed[:4], keyed[4]) if casts[1] is not None else (keyed, w_out[layer].astype(BF16))
    t_scan = _pick(seq, ROW_TILES[2:])
    ya, _ = _rwkv(z_rv, keyed, gate, par, s_meta.reshape(groups, GL, GL), bsz, t_scan, ng)
    yb = _conv_proj(u, w_all, c0, c_ga + d, conv_w[layer], m_conv, bsz, t_seq, tc, casts[2])
    yb, wu = yb if casts[2] is not None else (yb, w_up[layer].astype(BF16))
    tq = _pick(rows, ROW_TILES[1:])
    h1 = _out_proj(ya, yb, wo, xf, tq, _pick(d, COL_TILES[1:]))
    u2 = _rmsnorm(h1, norm_mlp_g[layer], BF16, min(tq, 256))
    hid = _mlp_up(u2, wu, tm, _pick(wu.shape[1], COL_TILES))
    h2 = _mlp_down(hid, wd, h1, tm, _pick(d, COL_TILES), _pick(wd.shape[0], (4096, 2048) + COL_TILES))
    out = _rmsnorm(h2, norm_final_g, F32, min(tq, 256))
    return out.reshape(bsz, seq, d)
```

```python
import functools
import math

import jax
import jax.numpy as jnp
from jax import lax
from jax.experimental import pallas as pl
from jax.experimental.pallas import tpu as pltpu

F32 = jnp.float32
BF16 = jnp.bfloat16

HEAD = 64
GROUP = 4
GL = GROUP * HEAD
CHUNK = 64
LORA = 128
NORM_EPS = 1e-6
GN_EPS = 64e-5
LOG2_DECAY_SCALE = -math.exp(-0.5) / math.log(2.0)
META_ROWS = 64
VMEM_LIMIT = 56 * 1024 * 1024
ROW_TILES = (1024, 512, 256, 128, 64)
COL_TILES = (1024, 512, 256, 128)


def _params(*sem):
    return pltpu.CompilerParams(dimension_semantics=sem, vmem_limit_bytes=VMEM_LIMIT)


def _sigmoid(x):
    return 0.5 * jnp.tanh(0.5 * x) + 0.5


def _ones_bd():
    return (lax.broadcasted_iota(jnp.int32, (GL, GL), 0) // HEAD
            == lax.broadcasted_iota(jnp.int32, (GL, GL), 1) // HEAD).astype(BF16)


def _with_cast(body, n_in, n_out):
    def kernel(*refs):
        src, dst = refs[n_in], refs[n_in + 1 + n_out]
        dst[...] = src[...].astype(dst.dtype)
        body(*refs[:n_in], *refs[n_in + 1:n_in + 1 + n_out], *refs[n_in + n_out + 2:])

    return kernel


def _cast_rows(src, steps):
    chunk = src.shape[0] // steps
    return chunk if src.shape[0] % steps == 0 and chunk % 16 == 0 else None


def _proj_call(body, n_col, n_row, in_specs, out_specs, out_shape, scratch, name, args, cast):
    if cast is not None:
        spec = pl.BlockSpec((_cast_rows(cast, n_col * n_row), cast.shape[1]), lambda j, i: (j * n_row + i, 0))
        body = _with_cast(body, len(in_specs), len(out_specs))
        in_specs, out_specs = [*in_specs, spec], [*out_specs, spec]
        out_shape, args = [*out_shape, jax.ShapeDtypeStruct(cast.shape, BF16)], (*args, cast)
    return pl.pallas_call(body, grid=(n_col, n_row), in_specs=in_specs, out_specs=out_specs, out_shape=out_shape,
                          scratch_shapes=scratch, compiler_params=_params("parallel", "arbitrary"), name=name)(*args)


def _rmsnorm_kernel(x_ref, g_ref, o_ref):
    x = x_ref[...]
    y = x * lax.rsqrt(jnp.mean(x * x, axis=-1, keepdims=True) + NORM_EPS)
    o_ref[...] = (y * g_ref[...]).astype(o_ref.dtype)


def _rmsnorm(x, g, out_dtype, tm):
    rows, d = x.shape
    return pl.pallas_call(
        _rmsnorm_kernel,
        grid=(rows // tm,),
        in_specs=[pl.BlockSpec((tm, d), lambda i: (i, 0)), pl.BlockSpec((1, d), lambda i: (0, 0))],
        out_specs=pl.BlockSpec((tm, d), lambda i: (i, 0)),
        out_shape=jax.ShapeDtypeStruct((rows, d), out_dtype),
        compiler_params=_params("parallel"),
        name="rmsnorm",
    )(x, g.reshape(1, d))


def _matmul_kernel(x_ref, w_ref, o_ref, *, gate):
    p = jnp.dot(x_ref[...], w_ref[...], preferred_element_type=F32)
    o_ref[...] = _sigmoid(p) if gate else p


def _matmul(x, w, tm, tn, gate=False):
    rows, k = x.shape
    n = w.shape[1]
    return pl.pallas_call(
        functools.partial(_matmul_kernel, gate=gate),
        grid=(rows // tm, n // tn),
        in_specs=[pl.BlockSpec((tm, k), lambda i, j: (i, 0)), pl.BlockSpec((k, tn), lambda i, j: (0, j))],
        out_specs=pl.BlockSpec((tm, tn), lambda i, j: (i, j)),
        out_shape=jax.ShapeDtypeStruct((rows, n), F32),
        compiler_params=_params("parallel", "parallel"),
        name="in_proj",
    )(x, w)


def _shift_proj_kernel(x_ref, w_ref, mu_ref, m_ref, o_ref, *rest, tile, tiles_per_seq):
    i = pl.program_id(1)
    carry = rest[-1]

    @pl.when(i % tiles_per_seq == 0)
    def _():
        carry[...] = m_ref[...]

    p = jnp.dot(x_ref[...], w_ref[...], preferred_element_type=F32)
    row = lax.broadcasted_iota(jnp.int32, p.shape, 0)
    prev = jnp.where(row == 0, carry[7:8, :], pltpu.roll(p, 1, 0))
    carry[...] = p[tile - 8:tile, :]
    o_ref[...] = p + (prev - p) * mu_ref[0:1, :]
    for raw_ref in rest[:-1]:
        raw_ref[...] = p


def _shift_proj(x, w, col_blk, mu, p_before, batch, tm, tn, cast=None, raw=False):
    rows, k = x.shape
    n = mu.shape[0]
    tiles_per_seq = rows // batch // tm
    m_blk = p_before.shape[0] // 8 - 1
    mu8 = jnp.zeros((8, n), F32).at[0].set(mu)
    out = _proj_call(
        functools.partial(_shift_proj_kernel, tile=tm, tiles_per_seq=tiles_per_seq), n // tn, rows // tm,
        [pl.BlockSpec((tm, k), lambda j, i: (i, 0)), pl.BlockSpec((k, tn), lambda j, i: (0, col_blk(j))),
         pl.BlockSpec((8, tn), lambda j, i: (0, j)), pl.BlockSpec((8, tn), lambda j, i: (m_blk, j))],
        [pl.BlockSpec((tm, tn), lambda j, i: (i, j))] * (1 + raw), [jax.ShapeDtypeStruct((rows, n), F32)] * (1 + raw),
        [pltpu.VMEM((8, tn), F32)], "shift_proj", (x, w, mu8, p_before), cast)
    return out[0] if len(out) == 1 else out


_K_MU, _K_W0, _K_A0, _K_KK, _K_KA = range(5)


def _key_proj_kernel(x_ref, w_ref, par_ref, m_ref, zl_ref, w2_ref, a2_ref, k_ref, kk_ref, ka_ref, lw_ref, *rest,
                     tile, tiles_per_seq):
    i = pl.program_id(1)
    carry = rest[-1]

    @pl.when(i % tiles_per_seq == 0)
    def _():
        carry[...] = m_ref[...]

    par = par_ref[...]
    prow = lambda r: par[r:r + 1, :]
    zw = jnp.tanh(zl_ref[:, :LORA]).astype(BF16)
    za = zl_ref[:, LORA:].astype(BF16)
    xw = jnp.dot(zw, w2_ref[...], preferred_element_type=F32)
    xa = jnp.dot(za, a2_ref[...], preferred_element_type=F32)
    p = jnp.dot(x_ref[...], w_ref[...], preferred_element_type=F32)
    lw_ref[...] = LOG2_DECAY_SCALE * _sigmoid(prow(_K_W0) + xw)
    a = _sigmoid(prow(_K_A0) + xa)
    row = lax.broadcasted_iota(jnp.int32, p.shape, 0)
    prev = jnp.where(row == 0, carry[7:8, :], pltpu.roll(p, 1, 0))
    carry[...] = p[tile - 8:tile, :]
    for raw_ref in rest[:-1]:
        raw_ref[...] = p
    z = p + (prev - p) * prow(_K_MU)
    k_ref[...] = z * (1.0 + (a - 1.0) * prow(_K_KA))
    kk = z * prow(_K_KK)
    sq = (kk * kk).astype(BF16)
    ones_bd = _ones_bd()
    ss = jnp.concatenate([jnp.dot(sq[:, c:c + GL], ones_bd, preferred_element_type=F32)
                          for c in range(0, sq.shape[1], GL)], axis=1)
    kk = kk * lax.rsqrt(jnp.maximum(ss, 1e-24))
    kk_ref[...] = kk
    ka_ref[...] = kk * a


def _key_proj(x, w, col_blk, par, p_before, z_lora, w2, a2, batch, tm, tn, cast=None, raw=False):
    rows, k = x.shape
    n = par.shape[1]
    tiles_per_seq = rows // batch // tm
    m_blk = p_before.shape[0] // 8 - 1
    return _proj_call(
        functools.partial(_key_proj_kernel, tile=tm, tiles_per_seq=tiles_per_seq), n // tn, rows // tm,
        [pl.BlockSpec((tm, k), lambda j, i: (i, 0)), pl.BlockSpec((k, tn), lambda j, i: (0, col_blk(j))),
         pl.BlockSpec((8, tn), lambda j, i: (0, j)), pl.BlockSpec((8, tn), lambda j, i: (m_blk, j)),
         pl.BlockSpec((tm, 2 * LORA), lambda j, i: (i, 0)),
         pl.BlockSpec((LORA, tn), lambda j, i: (0, j)), pl.BlockSpec((LORA, tn), lambda j, i: (0, j))],
        [pl.BlockSpec((tm, tn), lambda j, i: (i, j))] * (4 + raw), [jax.ShapeDtypeStruct((rows, n), F32)] * (4 + raw),
        [pltpu.VMEM((8, tn), F32)], "key_proj", (x, w, par, p_before, z_lora, w2, a2), cast)


def _out_proj_kernel(a_ref, b_ref, wa_ref, wb_ref, h_ref, o_ref):
    acc = jnp.dot(a_ref[...], wa_ref[...], preferred_element_type=F32)
    acc += jnp.dot(b_ref[...], wb_ref[...], preferred_element_type=F32)
    o_ref[...] = h_ref[...] + acc


def _out_proj(ya, yb, w, h, tm, tn):
    rows, k = ya.shape
    n = w.shape[1]
    return pl.pallas_call(
        _out_proj_kernel,
        grid=(rows // tm, n // tn),
        in_specs=[pl.BlockSpec((tm, k), lambda i, j: (i, 0)), pl.BlockSpec((tm, k), lambda i, j: (i, 0)),
                  pl.BlockSpec((k, tn), lambda i, j: (0, j)), pl.BlockSpec((k, tn), lambda i, j: (1, j)),
                  pl.BlockSpec((tm, tn), lambda i, j: (i, j))],
        out_specs=pl.BlockSpec((tm, tn), lambda i, j: (i, j)),
        out_shape=jax.ShapeDtypeStruct((rows, n), F32),
        compiler_params=_params("parallel", "parallel"),
        name="out_proj",
    )(ya, yb, w, w, h)


def _mlp_up_kernel(x_ref, w_ref, o_ref):
    hid = jnp.dot(x_ref[...], w_ref[...], preferred_element_type=F32)
    o_ref[...] = jnp.square(jnp.maximum(hid, 0.0)).astype(o_ref.dtype)


def _mlp_up(x, w, tm, tn):
    rows, k = x.shape
    n = w.shape[1]
    return pl.pallas_call(
        _mlp_up_kernel,
        grid=(rows // tm, n // tn),
        in_specs=[pl.BlockSpec((tm, k), lambda i, j: (i, 0)), pl.BlockSpec((k, tn), lambda i, j: (0, j))],
        out_specs=pl.BlockSpec((tm, tn), lambda i, j: (i, j)),
        out_shape=jax.ShapeDtypeStruct((rows, n), BF16),
        compiler_params=_params("parallel", "parallel"),
        name="mlp_up",
    )(x, w)


def _mlp_down_kernel(x_ref, w_ref, h_ref, o_ref):
    @pl.when(pl.program_id(2) == 0)
    def _():
        o_ref[...] = h_ref[...]

    o_ref[...] += jnp.dot(x_ref[...], w_ref[...], preferred_element_type=F32)


def _mlp_down(x, w, h, tm, tn, tk):
    rows, k = x.shape
    n = w.shape[1]
    return pl.pallas_call(
        _mlp_down_kernel,
        grid=(rows // tm, n // tn, k // tk),
        in_specs=[pl.BlockSpec((tm, tk), lambda i, j, q: (i, q)), pl.BlockSpec((tk, tn), lambda i, j, q: (q, j)),
                  pl.BlockSpec((tm, tn), lambda i, j, q: (i, j))],
        out_specs=pl.BlockSpec((tm, tn), lambda i, j, q: (i, j)),
        out_shape=jax.ShapeDtypeStruct((rows, n), F32),
        compiler_params=_params("parallel", "parallel", "arbitrary"),
        name="mlp_down",
    )(x, w, h)


_P_RK, _P_LNW, _P_LNB = range(3)
_P_ROWS = 8


def _rwkv_kernel(r_ref, k_ref, v_ref, kk_ref, ka_ref, lw_ref, gate_ref, par_ref, s0_ref,
                 y_ref, sout_ref, state, y_s, *, tile, ng):
    t_idx = pl.program_id(2)

    @pl.when(t_idx == 0)
    def _():
        state[...] = s0_ref[...]

    par = par_ref[0]
    prow = lambda i: par[i:i + 1, :]
    ones_bd = _ones_bd()

    def head_sum(xb):
        return jnp.concatenate([jnp.dot(xb[:, g * GL:(g + 1) * GL], ones_bd, preferred_element_type=F32)
                                for g in range(ng)], axis=1)

    ti = lax.broadcasted_iota(jnp.int32, (CHUNK, GL), 0)
    lane = lax.broadcasted_iota(jnp.int32, (CHUNK, GL), 1)
    si = lane % HEAD
    strict = si < ti
    incl = si <= ti
    eye = jnp.where(si == ti, 1.0, 0.0).astype(F32)
    head_of_lane = [lane // HEAD == h for h in range(GROUP)]
    bd_mask = (lax.broadcasted_iota(jnp.int32, (GL, GL), 0) // HEAD
               == lax.broadcasted_iota(jnp.int32, (GL, GL), 1) // HEAD)
    tri2 = (lax.broadcasted_iota(jnp.int32, (CHUNK, 2 * CHUNK), 1) % CHUNK
            <= lax.broadcasted_iota(jnp.int32, (CHUNK, 2 * CHUNK), 0)).astype(BF16)

    def bd(x):
        xb = x.astype(BF16)
        zero = jnp.zeros((), BF16)
        return jnp.concatenate([jnp.where(m, xb, zero) for m in head_of_lane], axis=0)

    def rob_mm(x, y):
        return jnp.dot(x.astype(BF16), bd(y), preferred_element_type=F32)

    def nt_dot(x, y):
        return lax.dot_general(x, y, (((1,), (1,)), ((), ())), preferred_element_type=F32)

    def group_chunk(rows, g):
        cols = slice(g * GL, (g + 1) * GL)
        rc, kc, vc, kkc, kka, lwc = (s[rows, cols] for s in (r_ref, k_ref, v_ref, kk_ref, ka_ref, lw_ref))
        hi = lwc.astype(BF16)
        lo = (lwc - hi.astype(F32)).astype(BF16)
        cum = jnp.dot(tri2, jnp.concatenate([hi, lo], axis=0), preferred_element_type=F32)
        yield
        cum_end = cum[CHUNK - 1:CHUNK, :]
        e_cum = jnp.exp2(cum)
        e_neg = jnp.exp2(-cum)
        e_prev = jnp.exp2(cum - lwc)
        e_end = jnp.exp2(cum_end - cum)
        al = -kkc * e_prev
        rt = rc * e_cum
        bt = kka * e_neg
        kt = kc * e_neg
        bh = kka * e_end
        kh = kc * e_end
        lr = jnp.concatenate([al, rt], axis=0).astype(BF16)
        x_b = nt_dot(lr, bd(bt))
        x_k = nt_dot(lr, bd(kt))
        s_old = state[g]
        ars = nt_dot(lr, s_old.astype(BF16))
        yield
        a_ab = jnp.where(strict, x_b[:CHUNK], 0.0)
        a_rb = jnp.where(incl, x_b[CHUNK:], 0.0)
        a_ak = jnp.where(strict, x_k[:CHUNK], 0.0)
        a_rk = jnp.where(incl, x_k[CHUNK:], 0.0)
        av = ars + rob_mm(jnp.concatenate([a_ak, a_rk], axis=0), vc)
        steps = int(math.log2(CHUNK))
        pw = rob_mm(a_ab, a_ab)
        tinv = eye + a_ab
        yield
        for _ in range(steps - 2):
            both = rob_mm(jnp.concatenate([pw, tinv], axis=0), pw)
            pw = both[:CHUNK]
            tinv = tinv + both[CHUNK:]
            yield
        tinv = tinv + rob_mm(tinv, pw)
        yield
        u = rob_mm(tinv, av[:CHUNK])
        yield
        y_s[rows, cols] = av[CHUNK:] + rob_mm(a_rb, u)
        uv = jnp.concatenate([u, vc], axis=0).astype(BF16)
        bk = jnp.concatenate([bh, kh], axis=0).astype(BF16)
        ds = lax.dot_general(uv, bk, (((0,), (0,)), ((), ())), preferred_element_type=F32)
        state[g] = s_old * jnp.exp2(cum_end) + jnp.where(bd_mask, ds, 0.0)

    def chunk_body(c, _):
        rows = pl.ds(pl.multiple_of(c * CHUNK, CHUNK), CHUNK)
        live = [group_chunk(rows, g) for g in range(ng)]
        while live:
            live = [gen for gen in live if next(gen, StopIteration) is not StopIteration]
        return 0

    lax.fori_loop(0, tile // CHUNK, chunk_body, 0)

    y = y_s[...]
    inv_n = 1.0 / HEAD
    y_hi = y.astype(BF16)
    y_lo = (y - y_hi.astype(F32)).astype(BF16)
    d = y - (head_sum(y_hi) + head_sum(y_lo)) * inv_n
    var = head_sum((d * d).astype(BF16)) * inv_n
    yn = d * lax.rsqrt(var + GN_EPS) * prow(_P_LNW) + prow(_P_LNB)
    bonus = head_sum((r_ref[...] * k_ref[...] * prow(_P_RK)).astype(BF16)) * v_ref[...]
    y_ref[...] = (gate_ref[...] * (yn + bonus)).astype(y_ref.dtype)

    @pl.when(t_idx == pl.num_programs(2) - 1)
    def _():
        sout_ref[0] = state[...]


def _rwkv(z_rv, keyed, gate, par, s0, batch, tile, ng):
    rows = z_rv.shape[0]
    d = z_rv.shape[1] // 2
    groups = d // GL
    sg = groups // ng
    width = ng * GL
    nt = rows // batch // tile
    wide = lambda arr: arr.reshape(sg, ng, arr.shape[1], GL).transpose(0, 2, 1, 3).reshape(sg, arr.shape[1], width)
    row_blk = lambda off: pl.BlockSpec((tile, width), lambda b, g, t: (b * nt + t, off + g))
    k, kk, ka, lw = keyed
    return pl.pallas_call(
        functools.partial(_rwkv_kernel, tile=tile, ng=ng),
        grid=(batch, sg, nt),
        in_specs=[row_blk(0), row_blk(0), row_blk(sg), row_blk(0), row_blk(0), row_blk(0), row_blk(0),
                  pl.BlockSpec((1, _P_ROWS, width), lambda b, g, t: (g, 0, 0)),
                  pl.BlockSpec((ng, GL, GL), lambda b, g, t: (g, 0, 0))],
        out_specs=[row_blk(0), pl.BlockSpec((1, ng, GL, GL), lambda b, g, t: (b * sg + g, 0, 0, 0))],
        out_shape=[jax.ShapeDtypeStruct((rows, d), BF16), jax.ShapeDtypeStruct((batch * sg, ng, GL, GL), F32)],
        scratch_shapes=[pltpu.VMEM((ng, GL, GL), F32), pltpu.VMEM((tile, width), F32)],
        compiler_params=_params("parallel", "parallel", "arbitrary"),
        name="rwkv7",
    )(z_rv, k, z_rv, kk, ka, lw, gate, wide(par), s0)


def _conv_proj_kernel(u_ref, wb_ref, wc_ref, wh_ref, wg_ref, cw_ref, mc_ref, mh_ref, o_ref, carry,
                      *, tile, tiles_per_seq):
    i = pl.program_id(1)

    @pl.when(i % tiles_per_seq == 0)
    def _():
        carry[...] = mc_ref[...] * mh_ref[...]

    x = u_ref[...]
    proj = lambda w_ref: jnp.dot(x, w_ref[...], preferred_element_type=F32)
    u = proj(wc_ref) * proj(wh_ref)
    row = lax.broadcasted_iota(jnp.int32, u.shape, 0)
    prev1 = jnp.where(row == 0, carry[7:8, :], pltpu.roll(u, 1, 0))
    prev2 = jnp.where(row == 0, carry[6:7, :], jnp.where(row == 1, carry[7:8, :], pltpu.roll(u, 2, 0)))
    carry[...] = u[tile - 8:tile, :]
    w = cw_ref[...]
    conv = w[0:1, :] * prev2 + w[1:2, :] * prev1 + w[2:3, :] * u
    o_ref[...] = (_sigmoid(proj(wg_ref)) * (proj(wb_ref) * conv)).astype(o_ref.dtype)


def _conv_proj(u, w, col0, gate_col0, conv_w, p_conv_meta, batch, tile, tc, cast=None):
    rows, k = u.shape
    d = conv_w.shape[1]
    nc = d // tc
    tiles_per_seq = rows // batch // tile
    meta_blk = p_conv_meta.shape[0] // 8 - 1
    wblk = lambda col: pl.BlockSpec((k, tc), lambda j, i: (0, col // tc + j))
    w8 = jnp.zeros((8, d), F32).at[:conv_w.shape[0]].set(conv_w)
    out = _proj_call(
        functools.partial(_conv_proj_kernel, tile=tile, tiles_per_seq=tiles_per_seq), nc, rows // tile,
        [pl.BlockSpec((tile, k), lambda j, i: (i, 0)),
         wblk(col0), wblk(col0 + d), wblk(col0 + 2 * d), wblk(gate_col0),
         pl.BlockSpec((8, tc), lambda j, i: (0, j)),
         pl.BlockSpec((8, tc), lambda j, i: (meta_blk, nc + j)),
         pl.BlockSpec((8, tc), lambda j, i: (meta_blk, 2 * nc + j))],
        [pl.BlockSpec((tile, tc), lambda j, i: (i, j))], [jax.ShapeDtypeStruct((rows, d), BF16)],
        [pltpu.VMEM((8, tc), F32)], "conv_proj", (u, w, w, w, w, w8, p_conv_meta, p_conv_meta), cast)
    return out[0] if cast is None else out


def _pick(n, prefs):
    for p in prefs:
        if n % p == 0:
            return p
    raise ValueError(f"no tile for {n} in {prefs}")


def kernel(x, meta_tokens, norm_mix_g, w_in, rwkv_shift_mu, rwkv_w0, rwkv_w2, rwkv_a0, rwkv_a2, rwkv_k_k, rwkv_k_a, rwkv_r_k, rwkv_ln_w, rwkv_ln_b, conv_w, w_out, norm_mlp_g, w_up, w_down, norm_final_g):
    bsz, seq, d = x.shape
    n_meta = meta_tokens.shape[0]
    depth = w_in.shape[0]
    assert depth == 1, "the meta-token hand-off is written for a single layer"
    assert d % GL == 0 and seq % CHUNK == 0 and n_meta <= META_ROWS
    groups = d // GL
    rows = bsz * seq
    layer = 0

    c_rkv, c_lora = 3 * d, 2 * LORA
    c0 = c_rkv + c_lora
    c_ga = c0 + 3 * d
    tc = _pick(d, (256, 128))
    assert c_rkv % c_lora == 0 and c0 % tc == 0, "column segments must start on block boundaries"
    w_all = w_in[layer].astype(BF16)
    w_ga = w_all[:, c_ga:c_ga + d]
    w2 = rwkv_w2[layer].astype(BF16)
    a2 = rwkv_a2[layer].astype(BF16)

    mu = rwkv_shift_mu[layer]
    mu_rv = jnp.concatenate([mu[:d], mu[2 * d:c_rkv]])
    rows8 = lambda vecs: jnp.zeros((8, d), F32).at[:len(vecs)].set(jnp.stack(vecs))
    key_par = rows8([mu[d:2 * d], rwkv_w0[layer], rwkv_a0[layer], rwkv_k_k[layer], rwkv_k_a[layer]])
    per_group = lambda vec: vec.reshape(groups, 1, GL)
    par = jnp.concatenate(
        [per_group(rwkv_r_k[layer].reshape(d)), per_group(rwkv_ln_w[layer]), per_group(rwkv_ln_b[layer]),
         jnp.zeros((groups, _P_ROWS - 3, GL), F32)], axis=1)
    tw = _pick(d, COL_TILES)
    tk = _pick(d, (512, 256))
    rv_blk = lambda j: j + (j // (d // tw)) * (d // tw)
    k_blk = lambda j: d // tk + j
    lora_blk = lambda j: c_rkv // c_lora + j

    def mixer_inputs(h_rows, batch, tm, before, casts=(None, None), raw=False):
        u = _rmsnorm(h_rows, norm_mix_g[layer], BF16, min(tm, 512))
        z_lora = _shift_proj(u, w_all, lora_blk, mu[c_rkv:c0], before[0], batch, tm, c_lora, raw=raw)
        z_rv = _shift_proj(u, w_all, rv_blk, mu_rv, before[1], batch, tm, tw, casts[0], raw)
        keyed = _key_proj(u, w_all, k_blk, key_par, before[2], z_lora[0] if raw else z_lora, w2, a2, batch, tm, tk,
                          casts[1], raw)
        return u, z_lora, z_rv, keyed, _matmul(u, w_ga, tm, _pick(d, COL_TILES), gate=True)

    meta = jnp.zeros((META_ROWS, d), F32).at[META_ROWS - n_meta:].set(meta_tokens.astype(F32))
    nothing = (jnp.zeros((8, c_lora), F32), jnp.zeros((8, 2 * d), F32), jnp.zeros((8, d), F32))
    u_meta, (_, m_lora), (mz_rv, m_rv), m_keyed, m_gate = mixer_inputs(meta, 1, META_ROWS, nothing, raw=True)
    ng = _pick(groups, (8, 4, 2, 1))
    _, s_meta = _rwkv(mz_rv, m_keyed[:4], m_gate, par, jnp.zeros((groups, GL, GL), F32), 1, META_ROWS, ng)

    xf = x.reshape(rows, d)
    tm = _pick(rows, ROW_TILES)
    t_seq = _pick(seq, ROW_TILES)
    m_conv = _shift_proj(u_meta, w_all, lambda j: c0 // tc + j, jnp.zeros((3 * d,), F32), jnp.zeros((8, 3 * d), F32),
                         1, META_ROWS, tc)
    n_row = rows // t_seq
    ride = lambda wgt, steps: wgt if _cast_rows(wgt, steps) else None
    casts = (ride(w_down[layer], 2 * d // tw * n_row), ride(w_out[layer], d // tk * n_row), ride(w_up[layer], d // tc * n_row))
    u, _, z_rv, keyed, gate = mixer_inputs(xf, bsz, t_seq, (m_lora, m_rv, m_keyed[4]), casts[:2])
    z_rv, wd = z_rv if casts[0] is not None else (z_rv, w_down[layer].astype(BF16))
    keyed, wo = (keyed[:4], keyed[4]) if casts[1] is not None else (keyed, w_out[layer].astype(BF16))
    t_scan = _pick(seq, ROW_TILES[2:])
    ya, _ = _rwkv(z_rv, keyed, gate, par, s_meta.reshape(groups, GL, GL), bsz, t_scan, ng)
    yb = _conv_proj(u, w_all, c0, c_ga + d, conv_w[layer], m_conv, bsz, t_seq, tc, casts[2])
    yb, wu = yb if casts[2] is not None else (yb, w_up[layer].astype(BF16))
    tq = _pick(rows, ROW_TILES[1:])
    h1 = _out_proj(ya, yb, wo, xf, tm, _pick(d, COL_TILES[2:]))
    u2 = _rmsnorm(h1, norm_mlp_g[layer], BF16, tq)
    hid = _mlp_up(u2, wu, tm, _pick(wu.shape[1], COL_TILES))
    h2 = _mlp_down(hid, wd, h1, tm, _pick(d, COL_TILES), _pick(wd.shape[0], (4096, 2048) + COL_TILES))
    out = _rmsnorm(h2, norm_final_g, F32, tq)
    return out.reshape(bsz, seq, d)
```

```python
import functools
import math

import jax
import jax.numpy as jnp
from jax import lax
from jax.experimental import pallas as pl
from jax.experimental.pallas import tpu as pltpu

F32 = jnp.float32
BF16 = jnp.bfloat16

HEAD = 64
GROUP = 4
GL = GROUP * HEAD
CHUNK = 64
LORA = 128
NORM_EPS = 1e-6
GN_EPS = 64e-5
LOG2_DECAY_SCALE = -math.exp(-0.5) / math.log(2.0)
META_ROWS = 64
VMEM_LIMIT = 56 * 1024 * 1024
ROW_TILES = (1024, 512, 256, 128, 64)
COL_TILES = (1024, 512, 256, 128)


def _params(*sem):
    return pltpu.CompilerParams(dimension_semantics=sem, vmem_limit_bytes=VMEM_LIMIT)


def _sigmoid(x):
    return 0.5 * jnp.tanh(0.5 * x) + 0.5


def _ones_bd():
    return (lax.broadcasted_iota(jnp.int32, (GL, GL), 0) // HEAD
            == lax.broadcasted_iota(jnp.int32, (GL, GL), 1) // HEAD).astype(BF16)


def _with_cast(body, n_in, n_out):
    def kernel(*refs):
        src, dst = refs[n_in], refs[n_in + 1 + n_out]
        dst[...] = src[...].astype(dst.dtype)
        body(*refs[:n_in], *refs[n_in + 1:n_in + 1 + n_out], *refs[n_in + n_out + 2:])

    return kernel


def _cast_rows(src, steps):
    chunk = src.shape[0] // steps
    return chunk if src.shape[0] % steps == 0 and chunk % 16 == 0 else None


def _proj_call(body, n_outer, n_inner, in_specs, out_specs, out_shape, scratch, name, args, cast):
    if cast is not None:
        src, col0, n_cols = cast
        chunk = _cast_rows(src, n_outer * n_inner)
        step = lambda a, b: a * n_inner + b
        body = _with_cast(body, len(in_specs), len(out_specs))
        in_specs = [*in_specs, pl.BlockSpec((pl.Element(chunk), pl.Element(n_cols)),
                                            lambda a, b: (step(a, b) * chunk, col0))]
        out_specs = [*out_specs, pl.BlockSpec((chunk, n_cols), lambda a, b: (step(a, b), 0))]
        out_shape, args = [*out_shape, jax.ShapeDtypeStruct((src.shape[0], n_cols), BF16)], (*args, src)
    return pl.pallas_call(body, grid=(n_outer, n_inner), in_specs=in_specs, out_specs=out_specs, out_shape=out_shape,
                          scratch_shapes=scratch, compiler_params=_params("parallel", "arbitrary"), name=name)(*args)


def _cast_kernel(x_ref, o_ref):
    o_ref[...] = x_ref[...].astype(o_ref.dtype)


def _cast_cols(src, n_cols, chunk):
    rows = src.shape[0]
    spec = pl.BlockSpec((chunk, n_cols), lambda i: (i, 0))
    return pl.pallas_call(_cast_kernel, grid=(rows // chunk,), in_specs=[spec], out_specs=spec,
                          out_shape=jax.ShapeDtypeStruct((rows, n_cols), BF16), compiler_params=_params("parallel"),
                          name="cast_cols")(src)


def _rmsnorm_kernel(x_ref, g_ref, o_ref):
    x = x_ref[...]
    y = x * lax.rsqrt(jnp.mean(x * x, axis=-1, keepdims=True) + NORM_EPS)
    o_ref[...] = (y * g_ref[...]).astype(o_ref.dtype)


def _rmsnorm(x, g, out_dtype, tm):
    rows, d = x.shape
    return pl.pallas_call(
        _rmsnorm_kernel,
        grid=(rows // tm,),
        in_specs=[pl.BlockSpec((tm, d), lambda i: (i, 0)), pl.BlockSpec((1, d), lambda i: (0, 0))],
        out_specs=pl.BlockSpec((tm, d), lambda i: (i, 0)),
        out_shape=jax.ShapeDtypeStruct((rows, d), out_dtype),
        compiler_params=_params("parallel"),
        name="rmsnorm",
    )(x, g.reshape(1, d))


def _matmul_kernel(x_ref, w_ref, o_ref, *, gate):
    p = jnp.dot(x_ref[...], w_ref[...], preferred_element_type=F32)
    o_ref[...] = _sigmoid(p) if gate else p


def _matmul(x, w, col0, n, tm, tn, gate=False):
    rows, k = x.shape
    return pl.pallas_call(
        functools.partial(_matmul_kernel, gate=gate),
        grid=(rows // tm, n // tn),
        in_specs=[pl.BlockSpec((tm, k), lambda i, j: (i, 0)), pl.BlockSpec((k, tn), lambda i, j: (0, col0 // tn + j))],
        out_specs=pl.BlockSpec((tm, tn), lambda i, j: (i, j)),
        out_shape=jax.ShapeDtypeStruct((rows, n), F32),
        compiler_params=_params("parallel", "parallel"),
        name="in_proj",
    )(x, w)


def _shift_proj_kernel(x_ref, w_ref, mu_ref, m_ref, o_ref, *rest, tile, tiles_per_seq):
    i = pl.program_id(1)
    carry = rest[-1]

    @pl.when(i % tiles_per_seq == 0)
    def _():
        carry[...] = m_ref[...]

    p = jnp.dot(x_ref[...], w_ref[...], preferred_element_type=F32)
    row = lax.broadcasted_iota(jnp.int32, p.shape, 0)
    prev = jnp.where(row == 0, carry[7:8, :], pltpu.roll(p, 1, 0))
    carry[...] = p[tile - 8:tile, :]
    o_ref[...] = p + (prev - p) * mu_ref[0:1, :]
    for raw_ref in rest[:-1]:
        raw_ref[...] = p


def _shift_proj(x, w, col_blk, mu, p_before, batch, tm, tn, cast=None, raw=False):
    rows, k = x.shape
    n = mu.shape[0]
    tiles_per_seq = rows // batch // tm
    m_blk = p_before.shape[0] // 8 - 1
    mu8 = jnp.zeros((8, n), F32).at[0].set(mu)
    out = _proj_call(
        functools.partial(_shift_proj_kernel, tile=tm, tiles_per_seq=tiles_per_seq), n // tn, rows // tm,
        [pl.BlockSpec((tm, k), lambda j, i: (i, 0)), pl.BlockSpec((k, tn), lambda j, i: (0, col_blk(j))),
         pl.BlockSpec((8, tn), lambda j, i: (0, j)), pl.BlockSpec((8, tn), lambda j, i: (m_blk, j))],
        [pl.BlockSpec((tm, tn), lambda j, i: (i, j))] * (1 + raw), [jax.ShapeDtypeStruct((rows, n), F32)] * (1 + raw),
        [pltpu.VMEM((8, tn), F32)], "shift_proj", (x, w, mu8, p_before), cast)
    return out[0] if len(out) == 1 else out


_K_MU, _K_W0, _K_A0, _K_KK, _K_KA = range(5)


def _key_proj_kernel(x_ref, w_ref, par_ref, m_ref, zl_ref, w2_ref, a2_ref, k_ref, kk_ref, ka_ref, lw_ref, *rest,
                     tile, tiles_per_seq):
    i = pl.program_id(1)
    carry = rest[-1]

    @pl.when(i % tiles_per_seq == 0)
    def _():
        carry[...] = m_ref[...]

    par = par_ref[...]
    prow = lambda r: par[r:r + 1, :]
    zw = jnp.tanh(zl_ref[:, :LORA]).astype(BF16)
    za = zl_ref[:, LORA:].astype(BF16)
    xw = jnp.dot(zw, w2_ref[...], preferred_element_type=F32)
    xa = jnp.dot(za, a2_ref[...], preferred_element_type=F32)
    p = jnp.dot(x_ref[...], w_ref[...], preferred_element_type=F32)
    lw_ref[...] = LOG2_DECAY_SCALE * _sigmoid(prow(_K_W0) + xw)
    a = _sigmoid(prow(_K_A0) + xa)
    row = lax.broadcasted_iota(jnp.int32, p.shape, 0)
    prev = jnp.where(row == 0, carry[7:8, :], pltpu.roll(p, 1, 0))
    carry[...] = p[tile - 8:tile, :]
    for raw_ref in rest[:-1]:
        raw_ref[...] = p
    z = p + (prev - p) * prow(_K_MU)
    k_ref[...] = z * (1.0 + (a - 1.0) * prow(_K_KA))
    kk = z * prow(_K_KK)
    sq = (kk * kk).astype(BF16)
    ones_bd = _ones_bd()
    ss = jnp.concatenate([jnp.dot(sq[:, c:c + GL], ones_bd, preferred_element_type=F32)
                          for c in range(0, sq.shape[1], GL)], axis=1)
    kk = kk * lax.rsqrt(jnp.maximum(ss, 1e-24))
    kk_ref[...] = kk
    ka_ref[...] = kk * a


def _key_proj(x, w, col_blk, par, p_before, z_lora, w2, a2, batch, tm, tn, cast=None, raw=False):
    rows, k = x.shape
    n = par.shape[1]
    tiles_per_seq = rows // batch // tm
    m_blk = p_before.shape[0] // 8 - 1
    return _proj_call(
        functools.partial(_key_proj_kernel, tile=tm, tiles_per_seq=tiles_per_seq), n // tn, rows // tm,
        [pl.BlockSpec((tm, k), lambda j, i: (i, 0)), pl.BlockSpec((k, tn), lambda j, i: (0, col_blk(j))),
         pl.BlockSpec((8, tn), lambda j, i: (0, j)), pl.BlockSpec((8, tn), lambda j, i: (m_blk, j)),
         pl.BlockSpec((tm, 2 * LORA), lambda j, i: (i, 0)),
         pl.BlockSpec((LORA, tn), lambda j, i: (0, j)), pl.BlockSpec((LORA, tn), lambda j, i: (0, j))],
        [pl.BlockSpec((tm, tn), lambda j, i: (i, j))] * (4 + raw), [jax.ShapeDtypeStruct((rows, n), F32)] * (4 + raw),
        [pltpu.VMEM((8, tn), F32)], "key_proj", (x, w, par, p_before, z_lora, w2, a2), cast)


def _out_proj_kernel(a_ref, b_ref, wa_ref, wb_ref, h_ref, o_ref):
    acc = jnp.dot(a_ref[...], wa_ref[...], preferred_element_type=F32)
    acc += jnp.dot(b_ref[...], wb_ref[...], preferred_element_type=F32)
    o_ref[...] = h_ref[...] + acc


def _out_proj(ya, yb, w, h, tm, tn, cast=None):
    rows, k = ya.shape
    n = w.shape[1]
    out = _proj_call(
        _out_proj_kernel, rows // tm, n // tn,
        [pl.BlockSpec((tm, k), lambda i, j: (i, 0)), pl.BlockSpec((tm, k), lambda i, j: (i, 0)),
         pl.BlockSpec((k, tn), lambda i, j: (0, j)), pl.BlockSpec((k, tn), lambda i, j: (1, j)),
         pl.BlockSpec((tm, tn), lambda i, j: (i, j))],
        [pl.BlockSpec((tm, tn), lambda i, j: (i, j))], [jax.ShapeDtypeStruct((rows, n), F32)],
        [], "out_proj", (ya, yb, w, w, h), cast)
    return out[0] if cast is None else out


def _mlp_up_kernel(x_ref, w_ref, o_ref):
    hid = jnp.dot(x_ref[...], w_ref[...], preferred_element_type=F32)
    o_ref[...] = jnp.square(jnp.maximum(hid, 0.0)).astype(o_ref.dtype)


def _mlp_up(x, w, tm, tn):
    rows, k = x.shape
    n = w.shape[1]
    return pl.pallas_call(
        _mlp_up_kernel,
        grid=(rows // tm, n // tn),
        in_specs=[pl.BlockSpec((tm, k), lambda i, j: (i, 0)), pl.BlockSpec((k, tn), lambda i, j: (0, j))],
        out_specs=pl.BlockSpec((tm, tn), lambda i, j: (i, j)),
        out_shape=jax.ShapeDtypeStruct((rows, n), BF16),
        compiler_params=_params("parallel", "parallel"),
        name="mlp_up",
    )(x, w)


def _mlp_down_kernel(x_ref, w_ref, h_ref, o_ref):
    @pl.when(pl.program_id(2) == 0)
    def _():
        o_ref[...] = h_ref[...]

    o_ref[...] += jnp.dot(x_ref[...], w_ref[...], preferred_element_type=F32)


def _mlp_down(x, w, h, tm, tn, tk):
    rows, k = x.shape
    n = w.shape[1]
    return pl.pallas_call(
        _mlp_down_kernel,
        grid=(rows // tm, n // tn, k // tk),
        in_specs=[pl.BlockSpec((tm, tk), lambda i, j, q: (i, q)), pl.BlockSpec((tk, tn), lambda i, j, q: (q, j)),
                  pl.BlockSpec((tm, tn), lambda i, j, q: (i, j))],
        out_specs=pl.BlockSpec((tm, tn), lambda i, j, q: (i, j)),
        out_shape=jax.ShapeDtypeStruct((rows, n), F32),
        compiler_params=_params("parallel", "parallel", "arbitrary"),
        name="mlp_down",
    )(x, w, h)


_P_RK, _P_LNW, _P_LNB = range(3)
_P_ROWS = 8


def _rwkv_kernel(r_ref, k_ref, v_ref, kk_ref, ka_ref, lw_ref, gate_ref, par_ref, s0_ref,
                 y_ref, sout_ref, state, y_s, *, tile, ng):
    t_idx = pl.program_id(2)

    @pl.when(t_idx == 0)
    def _():
        state[...] = s0_ref[...]

    par = par_ref[0]
    prow = lambda i: par[i:i + 1, :]
    ones_bd = _ones_bd()

    def head_sum(xb):
        return jnp.concatenate([jnp.dot(xb[:, g * GL:(g + 1) * GL], ones_bd, preferred_element_type=F32)
                                for g in range(ng)], axis=1)

    ti = lax.broadcasted_iota(jnp.int32, (CHUNK, GL), 0)
    lane = lax.broadcasted_iota(jnp.int32, (CHUNK, GL), 1)
    si = lane % HEAD
    strict = si < ti
    incl = si <= ti
    eye = jnp.where(si == ti, 1.0, 0.0).astype(F32)
    head_of_lane = [lane // HEAD == h for h in range(GROUP)]
    bd_mask = (lax.broadcasted_iota(jnp.int32, (GL, GL), 0) // HEAD
               == lax.broadcasted_iota(jnp.int32, (GL, GL), 1) // HEAD)
    tri2 = (lax.broadcasted_iota(jnp.int32, (CHUNK, 2 * CHUNK), 1) % CHUNK
            <= lax.broadcasted_iota(jnp.int32, (CHUNK, 2 * CHUNK), 0)).astype(BF16)

    def bd(x):
        xb = x.astype(BF16)
        zero = jnp.zeros((), BF16)
        return jnp.concatenate([jnp.where(m, xb, zero) for m in head_of_lane], axis=0)

    def rob_mm(x, y):
        return jnp.dot(x.astype(BF16), bd(y), preferred_element_type=F32)

    def nt_dot(x, y):
        return lax.dot_general(x, y, (((1,), (1,)), ((), ())), preferred_element_type=F32)

    def group_chunk(rows, g):
        cols = slice(g * GL, (g + 1) * GL)
        rc, kc, vc, kkc, kka, lwc = (s[rows, cols] for s in (r_ref, k_ref, v_ref, kk_ref, ka_ref, lw_ref))
        hi = lwc.astype(BF16)
        lo = (lwc - hi.astype(F32)).astype(BF16)
        cum = jnp.dot(tri2, jnp.concatenate([hi, lo], axis=0), preferred_element_type=F32)
        yield
        cum_end = cum[CHUNK - 1:CHUNK, :]
        e_cum = jnp.exp2(cum)
        e_neg = jnp.exp2(-cum)
        e_prev = jnp.exp2(cum - lwc)
        e_end = jnp.exp2(cum_end - cum)
        al = -kkc * e_prev
        rt = rc * e_cum
        bt = kka * e_neg
        kt = kc * e_neg
        bh = kka * e_end
        kh = kc * e_end
        lr = jnp.concatenate([al, rt], axis=0).astype(BF16)
        x_b = nt_dot(lr, bd(bt))
        x_k = nt_dot(lr, bd(kt))
        s_old = state[g]
        ars = nt_dot(lr, s_old.astype(BF16))
        yield
        a_ab = jnp.where(strict, x_b[:CHUNK], 0.0)
        a_rb = jnp.where(incl, x_b[CHUNK:], 0.0)
        a_ak = jnp.where(strict, x_k[:CHUNK], 0.0)
        a_rk = jnp.where(incl, x_k[CHUNK:], 0.0)
        av = ars + rob_mm(jnp.concatenate([a_ak, a_rk], axis=0), vc)
        steps = int(math.log2(CHUNK))
        pw = rob_mm(a_ab, a_ab)
        tinv = eye + a_ab
        yield
        for _ in range(steps - 2):
            both = rob_mm(jnp.concatenate([pw, tinv], axis=0), pw)
            pw = both[:CHUNK]
            tinv = tinv + both[CHUNK:]
            yield
        tinv = tinv + rob_mm(tinv, pw)
        yield
        u = rob_mm(tinv, av[:CHUNK])
        yield
        y_s[rows, cols] = av[CHUNK:] + rob_mm(a_rb, u)
        uv = jnp.concatenate([u, vc], axis=0).astype(BF16)
        bk = jnp.concatenate([bh, kh], axis=0).astype(BF16)
        ds = lax.dot_general(uv, bk, (((0,), (0,)), ((), ())), preferred_element_type=F32)
        state[g] = s_old * jnp.exp2(cum_end) + jnp.where(bd_mask, ds, 0.0)

    def chunk_body(c, _):
        rows = pl.ds(pl.multiple_of(c * CHUNK, CHUNK), CHUNK)
        live = [group_chunk(rows, g) for g in range(ng)]
        while live:
            live = [gen for gen in live if next(gen, StopIteration) is not StopIteration]
        return 0

    lax.fori_loop(0, tile // CHUNK, chunk_body, 0)

    y = y_s[...]
    inv_n = 1.0 / HEAD
    y_hi = y.astype(BF16)
    y_lo = (y - y_hi.astype(F32)).astype(BF16)
    d = y - (head_sum(y_hi) + head_sum(y_lo)) * inv_n
    var = head_sum((d * d).astype(BF16)) * inv_n
    yn = d * lax.rsqrt(var + GN_EPS) * prow(_P_LNW) + prow(_P_LNB)
    bonus = head_sum((r_ref[...] * k_ref[...] * prow(_P_RK)).astype(BF16)) * v_ref[...]
    y_ref[...] = (gate_ref[...] * (yn + bonus)).astype(y_ref.dtype)

    @pl.when(t_idx == pl.num_programs(2) - 1)
    def _():
        sout_ref[0] = state[...]


def _rwkv(z_rv, keyed, gate, par, s0, batch, tile, ng):
    rows = z_rv.shape[0]
    d = z_rv.shape[1] // 2
    groups = d // GL
    sg = groups // ng
    width = ng * GL
    nt = rows // batch // tile
    wide = lambda arr: arr.reshape(sg, ng, arr.shape[1], GL).transpose(0, 2, 1, 3).reshape(sg, arr.shape[1], width)
    row_blk = lambda off: pl.BlockSpec((tile, width), lambda b, g, t: (b * nt + t, off + g))
    k, kk, ka, lw = keyed
    return pl.pallas_call(
        functools.partial(_rwkv_kernel, tile=tile, ng=ng),
        grid=(batch, sg, nt),
        in_specs=[row_blk(0), row_blk(0), row_blk(sg), row_blk(0), row_blk(0), row_blk(0), row_blk(0),
                  pl.BlockSpec((1, _P_ROWS, width), lambda b, g, t: (g, 0, 0)),
                  pl.BlockSpec((ng, GL, GL), lambda b, g, t: (g, 0, 0))],
        out_specs=[row_blk(0), pl.BlockSpec((1, ng, GL, GL), lambda b, g, t: (b * sg + g, 0, 0, 0))],
        out_shape=[jax.ShapeDtypeStruct((rows, d), BF16), jax.ShapeDtypeStruct((batch * sg, ng, GL, GL), F32)],
        scratch_shapes=[pltpu.VMEM((ng, GL, GL), F32), pltpu.VMEM((tile, width), F32)],
        compiler_params=_params("parallel", "parallel", "arbitrary"),
        name="rwkv7",
    )(z_rv, k, z_rv, kk, ka, lw, gate, wide(par), s0)


def _conv_proj_kernel(u_ref, wb_ref, wc_ref, wh_ref, wg_ref, cw_ref, mc_ref, mh_ref, o_ref, carry,
                      *, tile, tiles_per_seq):
    i = pl.program_id(1)

    @pl.when(i % tiles_per_seq == 0)
    def _():
        carry[...] = mc_ref[...] * mh_ref[...]

    x = u_ref[...]
    proj = lambda w_ref: jnp.dot(x, w_ref[...], preferred_element_type=F32)
    u = proj(wc_ref) * proj(wh_ref)
    row = lax.broadcasted_iota(jnp.int32, u.shape, 0)
    prev1 = jnp.where(row == 0, carry[7:8, :], pltpu.roll(u, 1, 0))
    prev2 = jnp.where(row == 0, carry[6:7, :], jnp.where(row == 1, carry[7:8, :], pltpu.roll(u, 2, 0)))
    carry[...] = u[tile - 8:tile, :]
    w = cw_ref[...]
    conv = w[0:1, :] * prev2 + w[1:2, :] * prev1 + w[2:3, :] * u
    o_ref[...] = (_sigmoid(proj(wg_ref)) * (proj(wb_ref) * conv)).astype(o_ref.dtype)


def _conv_proj(u, w, col0, gate_col0, conv_w, p_conv_meta, batch, tile, tc, cast=None):
    rows, k = u.shape
    d = conv_w.shape[1]
    nc = d // tc
    tiles_per_seq = rows // batch // tile
    meta_blk = p_conv_meta.shape[0] // 8 - 1
    wblk = lambda col: pl.BlockSpec((k, tc), lambda j, i: (0, col // tc + j))
    w8 = jnp.zeros((8, d), F32).at[:conv_w.shape[0]].set(conv_w)
    out = _proj_call(
        functools.partial(_conv_proj_kernel, tile=tile, tiles_per_seq=tiles_per_seq), nc, rows // tile,
        [pl.BlockSpec((tile, k), lambda j, i: (i, 0)),
         wblk(col0), wblk(col0 + d), wblk(col0 + 2 * d), wblk(gate_col0),
         pl.BlockSpec((8, tc), lambda j, i: (0, j)),
         pl.BlockSpec((8, tc), lambda j, i: (meta_blk, nc + j)),
         pl.BlockSpec((8, tc), lambda j, i: (meta_blk, 2 * nc + j))],
        [pl.BlockSpec((tile, tc), lambda j, i: (i, j))], [jax.ShapeDtypeStruct((rows, d), BF16)],
        [pltpu.VMEM((8, tc), F32)], "conv_proj", (u, w, w, w, w, w8, p_conv_meta, p_conv_meta), cast)
    return out[0] if cast is None else out


def _pick(n, prefs):
    for p in prefs:
        if n % p == 0:
            return p
    raise ValueError(f"no tile for {n} in {prefs}")


def kernel(x, meta_tokens, norm_mix_g, w_in, rwkv_shift_mu, rwkv_w0, rwkv_w2, rwkv_a0, rwkv_a2, rwkv_k_k, rwkv_k_a, rwkv_r_k, rwkv_ln_w, rwkv_ln_b, conv_w, w_out, norm_mlp_g, w_up, w_down, norm_final_g):
    bsz, seq, d = x.shape
    n_meta = meta_tokens.shape[0]
    depth = w_in.shape[0]
    assert depth == 1, "the meta-token hand-off is written for a single layer"
    assert d % GL == 0 and seq % CHUNK == 0 and n_meta <= META_ROWS
    groups = d // GL
    rows = bsz * seq
    layer = 0

    c_rkv, c_lora = 3 * d, 2 * LORA
    c0 = c_rkv + c_lora
    n_all = w_in.shape[2]
    tc = _pick(d, (256, 128))
    assert c_rkv % c_lora == 0 and n_all == c0 + 5 * d, "unexpected column layout of w_in"
    w_a = _cast_cols(w_in[layer], c0, _pick(d, (128, 64, 32, 16)))
    w2 = rwkv_w2[layer].astype(BF16)
    a2 = rwkv_a2[layer].astype(BF16)

    mu = rwkv_shift_mu[layer]
    mu_rv = jnp.concatenate([mu[:d], mu[2 * d:c_rkv]])
    rows8 = lambda vecs: jnp.zeros((8, d), F32).at[:len(vecs)].set(jnp.stack(vecs))
    key_par = rows8([mu[d:2 * d], rwkv_w0[layer], rwkv_a0[layer], rwkv_k_k[layer], rwkv_k_a[layer]])
    per_group = lambda vec: vec.reshape(groups, 1, GL)
    par = jnp.concatenate(
        [per_group(rwkv_r_k[layer].reshape(d)), per_group(rwkv_ln_w[layer]), per_group(rwkv_ln_b[layer]),
         jnp.zeros((groups, _P_ROWS - 3, GL), F32)], axis=1)
    tw = _pick(d, COL_TILES)
    tk = _pick(d, (512, 256))
    rv_blk = lambda j: j + (j // (d // tw)) * (d // tw)
    k_blk = lambda j: d // tk + j
    lora_blk = lambda j: c_rkv // c_lora + j

    def mixer_inputs(h_rows, batch, tm, before, casts=(None, None), raw=False):
        u = _rmsnorm(h_rows, norm_mix_g[layer], BF16, min(tm, 512))
        z_lora = _shift_proj(u, w_a, lora_blk, mu[c_rkv:c0], before[0], batch, tm, c_lora, raw=raw)
        z_rv = _shift_proj(u, w_a, rv_blk, mu_rv, before[1], batch, tm, tw, casts[0], raw)
        keyed = _key_proj(u, w_a, k_blk, key_par, before[2], z_lora[0] if raw else z_lora, w2, a2, batch, tm, tk,
                          casts[1], raw)
        return u, z_lora, z_rv, keyed

    meta = jnp.zeros((META_ROWS, d), F32).at[META_ROWS - n_meta:].set(meta_tokens.astype(F32))
    nothing = (jnp.zeros((8, c_lora), F32), jnp.zeros((8, 2 * d), F32), jnp.zeros((8, d), F32))
    u_meta, (_, m_lora), (mz_rv, m_rv), m_keyed = mixer_inputs(meta, 1, META_ROWS, nothing, raw=True)
    ng = _pick(groups, (8, 4, 2, 1))
    _, s_meta = _rwkv(mz_rv, m_keyed[:4], jnp.zeros((META_ROWS, d), F32), par, jnp.zeros((groups, GL, GL), F32),
                      1, META_ROWS, ng)

    xf = x.reshape(rows, d)
    tm = _pick(rows, ROW_TILES)
    t_seq = _pick(seq, ROW_TILES)
    n_row = rows // t_seq
    tn_out = _pick(d, COL_TILES[2:])
    ride = lambda wgt, steps, col0=0, n_cols=None: ((wgt, col0, n_cols or wgt.shape[1])
                                                    if _cast_rows(wgt, steps) else None)
    casts = (ride(w_in[layer], 2 * d // tw * n_row, c0, n_all - c0), ride(w_out[layer], d // tk * n_row),
             ride(w_up[layer], d // tc * n_row), ride(w_down[layer], rows // tm * (d // tn_out)))
    u, _, z_rv, keyed = mixer_inputs(xf, bsz, t_seq, (m_lora, m_rv, m_keyed[4]), casts[:2])
    z_rv, w_b = z_rv if casts[0] is not None else (z_rv, w_in[layer, :, c0:].astype(BF16))
    keyed, wo = (keyed[:4], keyed[4]) if casts[1] is not None else (keyed, w_out[layer].astype(BF16))
    gate = _matmul(u, w_b, 3 * d, d, tm, _pick(d, COL_TILES), gate=True)
    t_scan = _pick(seq, ROW_TILES[2:])
    ya, _ = _rwkv(z_rv, keyed, gate, par, s_meta.reshape(groups, GL, GL), bsz, t_scan, ng)
    m_conv = _shift_proj(u_meta, w_b, lambda j: j, jnp.zeros((3 * d,), F32), jnp.zeros((8, 3 * d), F32),
                         1, META_ROWS, tc)
    yb = _conv_proj(u, w_b, 0, 4 * d, conv_w[layer], m_conv, bsz, t_seq, tc, casts[2])
    yb, wu = yb if casts[2] is not None else (yb, w_up[layer].astype(BF16))
    tq = _pick(rows, ROW_TILES[1:])
    h1 = _out_proj(ya, yb, wo, xf, tm, tn_out, casts[3])
    h1, wd = h1 if casts[3] is not None else (h1, w_down[layer].astype(BF16))
    u2 = _rmsnorm(h1, norm_mlp_g[layer], BF16, tq)
    hid = _mlp_up(u2, wu, tm, _pick(wu.shape[1], COL_TILES))
    h2 = _mlp_down(hid, wd, h1, tm, _pick(d, COL_TILES), _pick(wd.shape[0], (4096, 2048) + COL_TILES))
    out = _rmsnorm(h2, norm_final_g, F32, tq)
    return out.reshape(bsz, seq, d)
```

```python
import functools
import math

import jax
import jax.numpy as jnp
from jax import lax
from jax.experimental import pallas as pl
from jax.experimental.pallas import tpu as pltpu

F32 = jnp.float32
BF16 = jnp.bfloat16

HEAD = 64
GROUP = 4
GL = GROUP * HEAD
CHUNK = 64
LORA = 128
NORM_EPS = 1e-6
GN_EPS = 64e-5
LOG2_DECAY_SCALE = -math.exp(-0.5) / math.log(2.0)
META_ROWS = 64
VMEM_LIMIT = 56 * 1024 * 1024
ROW_TILES = (1024, 512, 256, 128, 64)
COL_TILES = (1024, 512, 256, 128)


def _params(*sem):
    return pltpu.CompilerParams(dimension_semantics=sem, vmem_limit_bytes=VMEM_LIMIT)


def _sigmoid(x):
    return 0.5 * jnp.tanh(0.5 * x) + 0.5


def _ones_bd():
    return (lax.broadcasted_iota(jnp.int32, (GL, GL), 0) // HEAD
            == lax.broadcasted_iota(jnp.int32, (GL, GL), 1) // HEAD).astype(BF16)


def _cast_kernel(x_ref, o_ref):
    o_ref[...] = x_ref[...].astype(o_ref.dtype)


def _rmsnorm_kernel(x_ref, g_ref, o_ref):
    x = x_ref[...]
    y = x * lax.rsqrt(jnp.mean(x * x, axis=-1, keepdims=True) + NORM_EPS)
    o_ref[...] = (y * g_ref[...]).astype(o_ref.dtype)


def _ride_cast(src, col0=0, n_cols=None):
    return dict(src=src, col0=col0, n_cols=n_cols or src.shape[1])


def _ride_norm(src, g, dtype, out_rows=None, out_row0=0):
    return dict(src=src, g=g, dtype=dtype, out_rows=out_rows or src.shape[0], out_row0=out_row0)


def _ride_ok(src, steps):
    return src.shape[0] % steps == 0 and src.shape[0] // steps % 16 == 0


def _call(body, grid, sem, in_specs, out_specs, out_shape, scratch, name, args, rides=()):
    n_in, n_out = len(in_specs), len(out_specs)
    in_specs, out_specs, out_shape, args = list(in_specs), list(out_specs), list(out_shape), list(args)

    def step(*ids):
        flat = ids[0]
        for n, i in zip(grid[1:], ids[1:]):
            flat = flat * n + i
        return flat

    jobs = []
    for job in rides:
        src = job["src"]
        assert _ride_ok(src, math.prod(grid)), (name, src.shape, grid)
        chunk = src.shape[0] // math.prod(grid)
        if "g" in job:
            d, blk0 = src.shape[1], job["out_row0"] // chunk
            in_specs += [pl.BlockSpec((chunk, d), lambda *ids: (step(*ids), 0)),
                         pl.BlockSpec((1, d), lambda *ids: (0, 0))]
            args += [src, job["g"].reshape(1, d)]
            out_specs.append(pl.BlockSpec((chunk, d), lambda *ids, blk0=blk0: (blk0 + step(*ids), 0)))
            out_shape.append(jax.ShapeDtypeStruct((job["out_rows"], d), job["dtype"]))
            jobs.append((2, _rmsnorm_kernel))
        else:
            col0, n_cols = job["col0"], job["n_cols"]
            in_specs.append(pl.BlockSpec((pl.Element(chunk), pl.Element(n_cols)),
                                         lambda *ids, chunk=chunk, col0=col0: (step(*ids) * chunk, col0)))
            args.append(src)
            out_specs.append(pl.BlockSpec((chunk, n_cols), lambda *ids: (step(*ids), 0)))
            out_shape.append(jax.ShapeDtypeStruct((src.shape[0], n_cols), BF16))
            jobs.append((1, _cast_kernel))

    def kernel(*refs):
        k = n_in
        ride_ins = []
        for n_refs, _ in jobs:
            ride_ins.append(refs[k:k + n_refs])
            k += n_refs
        ride_outs = refs[k + n_out:k + n_out + len(jobs)]
        for (_, fn), ins, out in zip(jobs, ride_ins, ride_outs):
            fn(*ins, out)
        body(*refs[:n_in], *refs[k:k + n_out], *refs[k + n_out + len(jobs):])

    return pl.pallas_call(kernel if jobs else body, grid=grid, in_specs=in_specs, out_specs=out_specs,
                          out_shape=out_shape, scratch_shapes=scratch, compiler_params=_params(*sem), name=name)(*args)


def _proj_call(body, n_outer, n_inner, in_specs, out_specs, out_shape, scratch, name, args, rides):
    return _call(body, (n_outer, n_inner), ("parallel", "arbitrary"), in_specs, out_specs, out_shape, scratch, name,
                 args, rides)


def _cast_cols(src, n_cols, chunk):
    rows = src.shape[0]
    spec = pl.BlockSpec((chunk, n_cols), lambda i: (i, 0))
    return pl.pallas_call(_cast_kernel, grid=(rows // chunk,), in_specs=[spec], out_specs=spec,
                          out_shape=jax.ShapeDtypeStruct((rows, n_cols), BF16), compiler_params=_params("parallel"),
                          name="cast_cols")(src)


def _rmsnorm(x, g, out_dtype, tm):
    rows, d = x.shape
    return pl.pallas_call(
        _rmsnorm_kernel,
        grid=(rows // tm,),
        in_specs=[pl.BlockSpec((tm, d), lambda i: (i, 0)), pl.BlockSpec((1, d), lambda i: (0, 0))],
        out_specs=pl.BlockSpec((tm, d), lambda i: (i, 0)),
        out_shape=jax.ShapeDtypeStruct((rows, d), out_dtype),
        compiler_params=_params("parallel"),
        name="rmsnorm",
    )(x, g.reshape(1, d))


def _rmsnorm_into(x, g, buf, row0, tm):
    rows, d = x.shape
    return pl.pallas_call(
        lambda x_ref, g_ref, buf_ref, o_ref: _rmsnorm_kernel(x_ref, g_ref, o_ref),
        grid=(rows // tm,),
        in_specs=[pl.BlockSpec((tm, d), lambda i: (i, 0)), pl.BlockSpec((1, d), lambda i: (0, 0)),
                  pl.BlockSpec(memory_space=pl.ANY)],
        out_specs=pl.BlockSpec((tm, d), lambda i: (row0 // tm + i, 0)),
        out_shape=jax.ShapeDtypeStruct(buf.shape, buf.dtype),
        input_output_aliases={2: 0},
        compiler_params=_params("parallel"),
        name="rmsnorm_into",
    )(x, g.reshape(1, d), buf)


def _matmul_kernel(x_ref, w_ref, o_ref, *, gate):
    p = jnp.dot(x_ref[...], w_ref[...], preferred_element_type=F32)
    o_ref[...] = _sigmoid(p) if gate else p


def _matmul(x, w, col0, n, tm, tn, gate=False):
    rows, k = x.shape
    return pl.pallas_call(
        functools.partial(_matmul_kernel, gate=gate),
        grid=(rows // tm, n // tn),
        in_specs=[pl.BlockSpec((tm, k), lambda i, j: (i, 0)), pl.BlockSpec((k, tn), lambda i, j: (0, col0 // tn + j))],
        out_specs=pl.BlockSpec((tm, tn), lambda i, j: (i, j)),
        out_shape=jax.ShapeDtypeStruct((rows, n), F32),
        compiler_params=_params("parallel", "parallel"),
        name="in_proj",
    )(x, w)


def _shift_proj_kernel(x_ref, w_ref, mu_ref, m_ref, o_ref, *rest, tile, tiles_per_seq):
    i = pl.program_id(1)
    carry = rest[-1]

    @pl.when(i % tiles_per_seq == 0)
    def _():
        carry[...] = m_ref[...]

    p = jnp.dot(x_ref[...], w_ref[...], preferred_element_type=F32)
    row = lax.broadcasted_iota(jnp.int32, p.shape, 0)
    prev = jnp.where(row == 0, carry[7:8, :], pltpu.roll(p, 1, 0))
    carry[...] = p[tile - 8:tile, :]
    o_ref[...] = p + (prev - p) * mu_ref[0:1, :]
    for raw_ref in rest[:-1]:
        raw_ref[...] = p


def _shift_proj(x, w, col_blk, mu, p_before, batch, tm, tn, rides=(), raw=False):
    rows, k = x.shape
    n = mu.shape[0]
    tiles_per_seq = rows // batch // tm
    m_blk = p_before.shape[0] // 8 - 1
    mu8 = jnp.zeros((8, n), F32).at[0].set(mu)
    out = _proj_call(
        functools.partial(_shift_proj_kernel, tile=tm, tiles_per_seq=tiles_per_seq), n // tn, rows // tm,
        [pl.BlockSpec((tm, k), lambda j, i: (i, 0)), pl.BlockSpec((k, tn), lambda j, i: (0, col_blk(j))),
         pl.BlockSpec((8, tn), lambda j, i: (0, j)), pl.BlockSpec((8, tn), lambda j, i: (m_blk, j))],
        [pl.BlockSpec((tm, tn), lambda j, i: (i, j))] * (1 + raw), [jax.ShapeDtypeStruct((rows, n), F32)] * (1 + raw),
        [pltpu.VMEM((8, tn), F32)], "shift_proj", (x, w, mu8, p_before), rides)
    return out[0] if len(out) == 1 else out


_K_MU, _K_W0, _K_A0, _K_KK, _K_KA = range(5)


def _key_proj_kernel(x_ref, w_ref, par_ref, m_ref, zl_ref, w2_ref, a2_ref, k_ref, kk_ref, ka_ref, lw_ref, *rest,
                     tile, tiles_per_seq):
    i = pl.program_id(1)
    carry = rest[-1]

    @pl.when(i % tiles_per_seq == 0)
    def _():
        carry[...] = m_ref[...]

    par = par_ref[...]
    prow = lambda r: par[r:r + 1, :]
    zw = jnp.tanh(zl_ref[:, :LORA]).astype(BF16)
    za = zl_ref[:, LORA:].astype(BF16)
    xw = jnp.dot(zw, w2_ref[...], preferred_element_type=F32)
    xa = jnp.dot(za, a2_ref[...], preferred_element_type=F32)
    p = jnp.dot(x_ref[...], w_ref[...], preferred_element_type=F32)
    lw_ref[...] = LOG2_DECAY_SCALE * _sigmoid(prow(_K_W0) + xw)
    a = _sigmoid(prow(_K_A0) + xa)
    row = lax.broadcasted_iota(jnp.int32, p.shape, 0)
    prev = jnp.where(row == 0, carry[7:8, :], pltpu.roll(p, 1, 0))
    carry[...] = p[tile - 8:tile, :]
    for raw_ref in rest[:-1]:
        raw_ref[...] = p
    z = p + (prev - p) * prow(_K_MU)
    k_ref[...] = z * (1.0 + (a - 1.0) * prow(_K_KA))
    kk = z * prow(_K_KK)
    sq = (kk * kk).astype(BF16)
    ones_bd = _ones_bd()
    ss = jnp.concatenate([jnp.dot(sq[:, c:c + GL], ones_bd, preferred_element_type=F32)
                          for c in range(0, sq.shape[1], GL)], axis=1)
    kk = kk * lax.rsqrt(jnp.maximum(ss, 1e-24))
    kk_ref[...] = kk
    ka_ref[...] = kk * a


def _key_proj(x, w, col_blk, par, p_before, z_lora, w2, a2, batch, tm, tn, rides=(), raw=False):
    rows, k = x.shape
    n = par.shape[1]
    tiles_per_seq = rows // batch // tm
    m_blk = p_before.shape[0] // 8 - 1
    return _proj_call(
        functools.partial(_key_proj_kernel, tile=tm, tiles_per_seq=tiles_per_seq), n // tn, rows // tm,
        [pl.BlockSpec((tm, k), lambda j, i: (i, 0)), pl.BlockSpec((k, tn), lambda j, i: (0, col_blk(j))),
         pl.BlockSpec((8, tn), lambda j, i: (0, j)), pl.BlockSpec((8, tn), lambda j, i: (m_blk, j)),
         pl.BlockSpec((tm, 2 * LORA), lambda j, i: (i, 0)),
         pl.BlockSpec((LORA, tn), lambda j, i: (0, j)), pl.BlockSpec((LORA, tn), lambda j, i: (0, j))],
        [pl.BlockSpec((tm, tn), lambda j, i: (i, j))] * (4 + raw), [jax.ShapeDtypeStruct((rows, n), F32)] * (4 + raw),
        [pltpu.VMEM((8, tn), F32)], "key_proj", (x, w, par, p_before, z_lora, w2, a2), rides)


def _out_proj_kernel(a_ref, b_ref, wa_ref, wb_ref, h_ref, o_ref):
    acc = jnp.dot(a_ref[...], wa_ref[...], preferred_element_type=F32)
    acc += jnp.dot(b_ref[...], wb_ref[...], preferred_element_type=F32)
    o_ref[...] = h_ref[...] + acc


def _out_proj(ya, yb, w, h, row0, rows, tm, tn, rides=()):
    k = ya.shape[1]
    n = w.shape[1]
    i0 = row0 // tm
    out = _call(
        _out_proj_kernel, (rows // tm, n // tn), ("parallel", "parallel"),
        [pl.BlockSpec((tm, k), lambda i, j: (i0 + i, 0)), pl.BlockSpec((tm, k), lambda i, j: (i0 + i, 0)),
         pl.BlockSpec((k, tn), lambda i, j: (0, j)), pl.BlockSpec((k, tn), lambda i, j: (1, j)),
         pl.BlockSpec((tm, tn), lambda i, j: (i0 + i, j))],
        [pl.BlockSpec((tm, tn), lambda i, j: (i, j))], [jax.ShapeDtypeStruct((rows, n), F32)],
        [], "out_proj", (ya, yb, w, w, h), rides)
    return out[0] if len(out) == 1 else out


def _mlp_up_kernel(x_ref, w_ref, o_ref):
    hid = jnp.dot(x_ref[...], w_ref[...], preferred_element_type=F32)
    o_ref[...] = jnp.square(jnp.maximum(hid, 0.0)).astype(o_ref.dtype)


def _mlp_up(x, w, tm, tn, rides=()):
    rows, k = x.shape
    n = w.shape[1]
    out = _call(
        _mlp_up_kernel, (rows // tm, n // tn), ("parallel", "parallel"),
        [pl.BlockSpec((tm, k), lambda i, j: (i, 0)), pl.BlockSpec((k, tn), lambda i, j: (0, j))],
        [pl.BlockSpec((tm, tn), lambda i, j: (i, j))], [jax.ShapeDtypeStruct((rows, n), BF16)],
        [], "mlp_up", (x, w), rides)
    return out[0] if len(out) == 1 else out


def _mlp_down_kernel(x_ref, w_ref, h_ref, o_ref):
    @pl.when(pl.program_id(2) == 0)
    def _():
        o_ref[...] = h_ref[...]

    o_ref[...] += jnp.dot(x_ref[...], w_ref[...], preferred_element_type=F32)


def _mlp_down(x, w, h, tm, tn, tk, rides=()):
    rows, k = x.shape
    n = w.shape[1]
    out = _call(
        _mlp_down_kernel, (rows // tm, n // tn, k // tk), ("parallel", "parallel", "arbitrary"),
        [pl.BlockSpec((tm, tk), lambda i, j, q: (i, q)), pl.BlockSpec((tk, tn), lambda i, j, q: (q, j)),
         pl.BlockSpec((tm, tn), lambda i, j, q: (i, j))],
        [pl.BlockSpec((tm, tn), lambda i, j, q: (i, j))], [jax.ShapeDtypeStruct((rows, n), F32)],
        [], "mlp_down", (x, w, h), rides)
    return out[0] if len(out) == 1 else out


_P_RK, _P_LNW, _P_LNB = range(3)
_P_ROWS = 8


def _rwkv_kernel(r_ref, k_ref, v_ref, kk_ref, ka_ref, lw_ref, gate_ref, par_ref, s0_ref,
                 y_ref, sout_ref, state, y_s, *, tile, ng):
    t_idx = pl.program_id(2)

    @pl.when(t_idx == 0)
    def _():
        state[...] = s0_ref[...]

    par = par_ref[0]
    prow = lambda i: par[i:i + 1, :]
    ones_bd = _ones_bd()

    def head_sum(xb):
        return jnp.concatenate([jnp.dot(xb[:, g * GL:(g + 1) * GL], ones_bd, preferred_element_type=F32)
                                for g in range(ng)], axis=1)

    ti = lax.broadcasted_iota(jnp.int32, (CHUNK, GL), 0)
    lane = lax.broadcasted_iota(jnp.int32, (CHUNK, GL), 1)
    si = lane % HEAD
    strict = si < ti
    incl = si <= ti
    eye = jnp.where(si == ti, 1.0, 0.0).astype(F32)
    head_of_lane = [lane // HEAD == h for h in range(GROUP)]
    bd_mask = (lax.broadcasted_iota(jnp.int32, (GL, GL), 0) // HEAD
               == lax.broadcasted_iota(jnp.int32, (GL, GL), 1) // HEAD)
    tri2 = (lax.broadcasted_iota(jnp.int32, (CHUNK, 2 * CHUNK), 1) % CHUNK
            <= lax.broadcasted_iota(jnp.int32, (CHUNK, 2 * CHUNK), 0)).astype(BF16)

    def bd(x):
        xb = x.astype(BF16)
        zero = jnp.zeros((), BF16)
        return jnp.concatenate([jnp.where(m, xb, zero) for m in head_of_lane], axis=0)

    def rob_mm(x, y):
        return jnp.dot(x.astype(BF16), bd(y), preferred_element_type=F32)

    def nt_dot(x, y):
        return lax.dot_general(x, y, (((1,), (1,)), ((), ())), preferred_element_type=F32)

    def group_chunk(rows, g):
        cols = slice(g * GL, (g + 1) * GL)
        rc, kc, vc, kkc, kka, lwc = (s[rows, cols] for s in (r_ref, k_ref, v_ref, kk_ref, ka_ref, lw_ref))
        hi = lwc.astype(BF16)
        lo = (lwc - hi.astype(F32)).astype(BF16)
        cum = jnp.dot(tri2, jnp.concatenate([hi, lo], axis=0), preferred_element_type=F32)
        yield
        cum_end = cum[CHUNK - 1:CHUNK, :]
        e_cum = jnp.exp2(cum)
        e_neg = jnp.exp2(-cum)
        e_prev = jnp.exp2(cum - lwc)
        e_end = jnp.exp2(cum_end - cum)
        al = -kkc * e_prev
        rt = rc * e_cum
        bt = kka * e_neg
        kt = kc * e_neg
        bh = kka * e_end
        kh = kc * e_end
        lr = jnp.concatenate([al, rt], axis=0).astype(BF16)
        x_b = nt_dot(lr, bd(bt))
        x_k = nt_dot(lr, bd(kt))
        s_old = state[g]
        ars = nt_dot(lr, s_old.astype(BF16))
        yield
        a_ab = jnp.where(strict, x_b[:CHUNK], 0.0)
        a_rb = jnp.where(incl, x_b[CHUNK:], 0.0)
        a_ak = jnp.where(strict, x_k[:CHUNK], 0.0)
        a_rk = jnp.where(incl, x_k[CHUNK:], 0.0)
        av = ars + rob_mm(jnp.concatenate([a_ak, a_rk], axis=0), vc)
        steps = int(math.log2(CHUNK))
        pw = rob_mm(a_ab, a_ab)
        tinv = eye + a_ab
        yield
        for _ in range(steps - 2):
            both = rob_mm(jnp.concatenate([pw, tinv], axis=0), pw)
            pw = both[:CHUNK]
            tinv = tinv + both[CHUNK:]
            yield
        tinv = tinv + rob_mm(tinv, pw)
        yield
        u = rob_mm(tinv, av[:CHUNK])
        yield
        y_s[rows, cols] = av[CHUNK:] + rob_mm(a_rb, u)
        uv = jnp.concatenate([u, vc], axis=0).astype(BF16)
        bk = jnp.concatenate([bh, kh], axis=0).astype(BF16)
        ds = lax.dot_general(uv, bk, (((0,), (0,)), ((), ())), preferred_element_type=F32)
        state[g] = s_old * jnp.exp2(cum_end) + jnp.where(bd_mask, ds, 0.0)

    def chunk_body(c, _):
        rows = pl.ds(pl.multiple_of(c * CHUNK, CHUNK), CHUNK)
        live = [group_chunk(rows, g) for g in range(ng)]
        while live:
            live = [gen for gen in live if next(gen, StopIteration) is not StopIteration]
        return 0

    lax.fori_loop(0, tile // CHUNK, chunk_body, 0)

    y = y_s[...]
    inv_n = 1.0 / HEAD
    y_hi = y.astype(BF16)
    y_lo = (y - y_hi.astype(F32)).astype(BF16)
    d = y - (head_sum(y_hi) + head_sum(y_lo)) * inv_n
    var = head_sum((d * d).astype(BF16)) * inv_n
    yn = d * lax.rsqrt(var + GN_EPS) * prow(_P_LNW) + prow(_P_LNB)
    bonus = head_sum((r_ref[...] * k_ref[...] * prow(_P_RK)).astype(BF16)) * v_ref[...]
    y_ref[...] = (gate_ref[...] * (yn + bonus)).astype(y_ref.dtype)

    @pl.when(t_idx == pl.num_programs(2) - 1)
    def _():
        sout_ref[0] = state[...]


def _rwkv(z_rv, keyed, gate, par, s0, batch, tile, ng):
    rows = z_rv.shape[0]
    d = z_rv.shape[1] // 2
    groups = d // GL
    sg = groups // ng
    width = ng * GL
    nt = rows // batch // tile
    wide = lambda arr: arr.reshape(sg, ng, arr.shape[1], GL).transpose(0, 2, 1, 3).reshape(sg, arr.shape[1], width)
    row_blk = lambda off: pl.BlockSpec((tile, width), lambda b, g, t: (b * nt + t, off + g))
    k, kk, ka, lw = keyed
    return pl.pallas_call(
        functools.partial(_rwkv_kernel, tile=tile, ng=ng),
        grid=(batch, sg, nt),
        in_specs=[row_blk(0), row_blk(0), row_blk(sg), row_blk(0), row_blk(0), row_blk(0), row_blk(0),
                  pl.BlockSpec((1, _P_ROWS, width), lambda b, g, t: (g, 0, 0)),
                  pl.BlockSpec((ng, GL, GL), lambda b, g, t: (g, 0, 0))],
        out_specs=[row_blk(0), pl.BlockSpec((1, ng, GL, GL), lambda b, g, t: (b * sg + g, 0, 0, 0))],
        out_shape=[jax.ShapeDtypeStruct((rows, d), BF16), jax.ShapeDtypeStruct((batch * sg, ng, GL, GL), F32)],
        scratch_shapes=[pltpu.VMEM((ng, GL, GL), F32), pltpu.VMEM((tile, width), F32)],
        compiler_params=_params("parallel", "parallel", "arbitrary"),
        name="rwkv7",
    )(z_rv, k, z_rv, kk, ka, lw, gate, wide(par), s0)


def _conv_proj_kernel(u_ref, wb_ref, wc_ref, wh_ref, wg_ref, cw_ref, mc_ref, mh_ref, o_ref, carry,
                      *, tile, tiles_per_seq):
    i = pl.program_id(1)

    @pl.when(i % tiles_per_seq == 0)
    def _():
        carry[...] = mc_ref[...] * mh_ref[...]

    x = u_ref[...]
    proj = lambda w_ref: jnp.dot(x, w_ref[...], preferred_element_type=F32)
    u = proj(wc_ref) * proj(wh_ref)
    row = lax.broadcasted_iota(jnp.int32, u.shape, 0)
    prev1 = jnp.where(row == 0, carry[7:8, :], pltpu.roll(u, 1, 0))
    prev2 = jnp.where(row == 0, carry[6:7, :], jnp.where(row == 1, carry[7:8, :], pltpu.roll(u, 2, 0)))
    carry[...] = u[tile - 8:tile, :]
    w = cw_ref[...]
    conv = w[0:1, :] * prev2 + w[1:2, :] * prev1 + w[2:3, :] * u
    o_ref[...] = (_sigmoid(proj(wg_ref)) * (proj(wb_ref) * conv)).astype(o_ref.dtype)


def _conv_proj(u, w, col0, gate_col0, conv_w, p_conv_meta, batch, tile, tc, rides=()):
    rows, k = u.shape
    d = conv_w.shape[1]
    nc = d // tc
    tiles_per_seq = rows // batch // tile
    meta_blk = p_conv_meta.shape[0] // 8 - 1
    wblk = lambda col: pl.BlockSpec((k, tc), lambda j, i: (0, col // tc + j))
    w8 = jnp.zeros((8, d), F32).at[:conv_w.shape[0]].set(conv_w)
    out = _proj_call(
        functools.partial(_conv_proj_kernel, tile=tile, tiles_per_seq=tiles_per_seq), nc, rows // tile,
        [pl.BlockSpec((tile, k), lambda j, i: (i, 0)),
         wblk(col0), wblk(col0 + d), wblk(col0 + 2 * d), wblk(gate_col0),
         pl.BlockSpec((8, tc), lambda j, i: (0, j)),
         pl.BlockSpec((8, tc), lambda j, i: (meta_blk, nc + j)),
         pl.BlockSpec((8, tc), lambda j, i: (meta_blk, 2 * nc + j))],
        [pl.BlockSpec((tile, tc), lambda j, i: (i, j))], [jax.ShapeDtypeStruct((rows, d), BF16)],
        [pltpu.VMEM((8, tc), F32)], "conv_proj", (u, w, w, w, w, w8, p_conv_meta, p_conv_meta), rides)
    return out[0] if len(out) == 1 else out


def _pick(n, prefs):
    for p in prefs:
        if n % p == 0:
            return p
    raise ValueError(f"no tile for {n} in {prefs}")


def kernel(x, meta_tokens, norm_mix_g, w_in, rwkv_shift_mu, rwkv_w0, rwkv_w2, rwkv_a0, rwkv_a2, rwkv_k_k, rwkv_k_a, rwkv_r_k, rwkv_ln_w, rwkv_ln_b, conv_w, w_out, norm_mlp_g, w_up, w_down, norm_final_g):
    bsz, seq, d = x.shape
    n_meta = meta_tokens.shape[0]
    depth = w_in.shape[0]
    assert depth == 1, "the meta-token hand-off is written for a single layer"
    assert d % GL == 0 and seq % CHUNK == 0 and n_meta <= META_ROWS
    groups = d // GL
    rows = bsz * seq
    layer = 0

    c_rkv, c_lora = 3 * d, 2 * LORA
    c0 = c_rkv + c_lora
    n_all = w_in.shape[2]
    tc = _pick(d, (256, 128))
    assert c_rkv % c_lora == 0 and n_all == c0 + 5 * d, "unexpected column layout of w_in"
    w_a = _cast_cols(w_in[layer], c0, _pick(d, (128, 64, 32, 16)))
    w2 = rwkv_w2[layer].astype(BF16)
    a2 = rwkv_a2[layer].astype(BF16)

    mu = rwkv_shift_mu[layer]
    mu_rv = jnp.concatenate([mu[:d], mu[2 * d:c_rkv]])
    rows8 = lambda vecs: jnp.zeros((8, d), F32).at[:len(vecs)].set(jnp.stack(vecs))
    key_par = rows8([mu[d:2 * d], rwkv_w0[layer], rwkv_a0[layer], rwkv_k_k[layer], rwkv_k_a[layer]])
    per_group = lambda vec: vec.reshape(groups, 1, GL)
    par = jnp.concatenate(
        [per_group(rwkv_r_k[layer].reshape(d)), per_group(rwkv_ln_w[layer]), per_group(rwkv_ln_b[layer]),
         jnp.zeros((groups, _P_ROWS - 3, GL), F32)], axis=1)
    tw = _pick(d, COL_TILES)
    tk = _pick(d, (512, 256))
    rv_blk = lambda j: j + (j // (d // tw)) * (d // tw)
    k_blk = lambda j: d // tk + j
    lora_blk = lambda j: c_rkv // c_lora + j

    def mixer_inputs(h_rows, batch, tm, before, casts=((), ()), raw=False):
        u = _rmsnorm(h_rows, norm_mix_g[layer], BF16, min(tm, 512))
        z_lora = _shift_proj(u, w_a, lora_blk, mu[c_rkv:c0], before[0], batch, tm, c_lora, raw=raw)
        z_rv = _shift_proj(u, w_a, rv_blk, mu_rv, before[1], batch, tm, tw, casts[0], raw)
        keyed = _key_proj(u, w_a, k_blk, key_par, before[2], z_lora[0] if raw else z_lora, w2, a2, batch, tm, tk,
                          casts[1], raw)
        return u, z_lora, z_rv, keyed

    meta = jnp.zeros((META_ROWS, d), F32).at[META_ROWS - n_meta:].set(meta_tokens.astype(F32))
    nothing = (jnp.zeros((8, c_lora), F32), jnp.zeros((8, 2 * d), F32), jnp.zeros((8, d), F32))
    u_meta, (_, m_lora), (mz_rv, m_rv), m_keyed = mixer_inputs(meta, 1, META_ROWS, nothing, raw=True)
    ng = _pick(groups, (8, 4, 2, 1))
    _, s_meta = _rwkv(mz_rv, m_keyed[:4], jnp.zeros((META_ROWS, d), F32), par, jnp.zeros((groups, GL, GL), F32),
                      1, META_ROWS, ng)

    xf = x.reshape(rows, d)
    tm = _pick(rows, ROW_TILES)
    t_seq = _pick(seq, ROW_TILES)
    n_row = rows // t_seq
    assert bsz % 2 == 0, "the tail of the block alternates between two halves of the batch"
    half = rows // 2
    th = _pick(half, ROW_TILES)
    tn_out = _pick(d, COL_TILES[2:])
    ride = lambda job, steps: [job] if _ride_ok(job["src"], steps) else []
    casts = (ride(_ride_cast(w_in[layer], c0, n_all - c0), 2 * d // tw * n_row),
             ride(_ride_cast(w_out[layer]), d // tk * n_row), ride(_ride_cast(w_up[layer]), d // tc * n_row),
             ride(_ride_cast(w_down[layer]), half // th * (d // tn_out)))
    u, _, z_rv, keyed = mixer_inputs(xf, bsz, t_seq, (m_lora, m_rv, m_keyed[4]), casts[:2])
    z_rv, w_b = z_rv if casts[0] else (z_rv, w_in[layer, :, c0:].astype(BF16))
    keyed, wo = (keyed[:4], keyed[4]) if casts[1] else (keyed, w_out[layer].astype(BF16))
    gate = _matmul(u, w_b, 3 * d, d, tm, _pick(d, COL_TILES), gate=True)
    t_scan = _pick(seq, ROW_TILES[2:])
    ya, _ = _rwkv(z_rv, keyed, gate, par, s_meta.reshape(groups, GL, GL), bsz, t_scan, ng)
    m_conv = _shift_proj(u_meta, w_b, lambda j: j, jnp.zeros((3 * d,), F32), jnp.zeros((8, 3 * d), F32),
                         1, META_ROWS, tc)
    yb = _conv_proj(u, w_b, 0, 4 * d, conv_w[layer], m_conv, bsz, t_seq, tc, casts[2])
    yb, wu = yb if casts[2] else (yb, w_up[layer].astype(BF16))
    g_mlp, g_out = norm_mlp_g[layer], norm_final_g
    tn_up, tk_down = _pick(wu.shape[1], COL_TILES), _pick(wu.shape[1], (4096, 2048) + COL_TILES)
    h1_a = _out_proj(ya, yb, wo, xf, 0, half, th, tn_out, casts[3])
    h1_a, wd = h1_a if casts[3] else (h1_a, w_down[layer].astype(BF16))
    h1_b, u2_a = _out_proj(ya, yb, wo, xf, half, half, th, tn_out, [_ride_norm(h1_a, g_mlp, BF16)])
    hid_a, u2_b = _mlp_up(u2_a, wu, th, tn_up, [_ride_norm(h1_b, g_mlp, BF16)])
    hid_b = _mlp_up(u2_b, wu, th, tn_up)
    h2_a = _mlp_down(hid_a, wd, h1_a, th, _pick(d, COL_TILES), tk_down)
    h2_b, out = _mlp_down(hid_b, wd, h1_b, th, _pick(d, COL_TILES[1:]), tk_down, [_ride_norm(h2_a, g_out, F32, rows)])
    out = _rmsnorm_into(h2_b, g_out, out, half, _pick(half, ROW_TILES[1:]))
    return out.reshape(bsz, seq, d)
```

```python
import functools
import math

import jax
import jax.numpy as jnp
from jax import lax
from jax.experimental import pallas as pl
from jax.experimental.pallas import tpu as pltpu

F32 = jnp.float32
BF16 = jnp.bfloat16

HEAD = 64
GROUP = 4
GL = GROUP * HEAD
CHUNK = 64
LORA = 128
NORM_EPS = 1e-6
GN_EPS = 64e-5
LOG2_DECAY_SCALE = -math.exp(-0.5) / math.log(2.0)
META_ROWS = 64
VMEM_LIMIT = 56 * 1024 * 1024
ROW_TILES = (1024, 512, 256, 128, 64)
COL_TILES = (1024, 512, 256, 128)


def _params(*sem):
    return pltpu.CompilerParams(dimension_semantics=sem, vmem_limit_bytes=VMEM_LIMIT)


def _sigmoid(x):
    return 0.5 * jnp.tanh(0.5 * x) + 0.5


def _ones_bd():
    return (lax.broadcasted_iota(jnp.int32, (GL, GL), 0) // HEAD
            == lax.broadcasted_iota(jnp.int32, (GL, GL), 1) // HEAD).astype(BF16)


def _cast_kernel(x_ref, o_ref):
    o_ref[...] = x_ref[...].astype(o_ref.dtype)


def _rmsnorm_kernel(x_ref, g_ref, o_ref):
    x = x_ref[...]
    y = x * lax.rsqrt(jnp.mean(x * x, axis=-1, keepdims=True) + NORM_EPS)
    o_ref[...] = (y * g_ref[...]).astype(o_ref.dtype)


def _ride_cast(src, col0=0, n_cols=None):
    return dict(src=src, col0=col0, n_cols=n_cols or src.shape[1])


def _ride_norm(src, g, dtype):
    return dict(src=src, g=g, dtype=dtype)


def _ride_ok(src, steps):
    return src.shape[0] % steps == 0 and src.shape[0] // steps % 16 == 0


def _call(body, grid, sem, in_specs, out_specs, out_shape, scratch, name, args, rides=()):
    n_in, n_out = len(in_specs), len(out_specs)
    in_specs, out_specs, out_shape, args = list(in_specs), list(out_specs), list(out_shape), list(args)

    def step(*ids):
        flat = ids[0]
        for n, i in zip(grid[1:], ids[1:]):
            flat = flat * n + i
        return flat

    jobs = []
    for job in rides:
        src = job["src"]
        assert _ride_ok(src, math.prod(grid)), (name, src.shape, grid)
        chunk = src.shape[0] // math.prod(grid)
        if "g" in job:
            d = src.shape[1]
            in_specs += [pl.BlockSpec((chunk, d), lambda *ids: (step(*ids), 0)),
                         pl.BlockSpec((1, d), lambda *ids: (0, 0))]
            args += [src, job["g"].reshape(1, d)]
            out_specs.append(pl.BlockSpec((chunk, d), lambda *ids: (step(*ids), 0)))
            out_shape.append(jax.ShapeDtypeStruct(src.shape, job["dtype"]))
            jobs.append((2, _rmsnorm_kernel))
        else:
            col0, n_cols = job["col0"], job["n_cols"]
            in_specs.append(pl.BlockSpec((pl.Element(chunk), pl.Element(n_cols)),
                                         lambda *ids, chunk=chunk, col0=col0: (step(*ids) * chunk, col0)))
            args.append(src)
            out_specs.append(pl.BlockSpec((chunk, n_cols), lambda *ids: (step(*ids), 0)))
            out_shape.append(jax.ShapeDtypeStruct((src.shape[0], n_cols), BF16))
            jobs.append((1, _cast_kernel))

    def kernel(*refs):
        k = n_in
        ride_ins = []
        for n_refs, _ in jobs:
            ride_ins.append(refs[k:k + n_refs])
            k += n_refs
        ride_outs = refs[k + n_out:k + n_out + len(jobs)]
        for (_, fn), ins, out in zip(jobs, ride_ins, ride_outs):
            fn(*ins, out)
        body(*refs[:n_in], *refs[k:k + n_out], *refs[k + n_out + len(jobs):])

    return pl.pallas_call(kernel if jobs else body, grid=grid, in_specs=in_specs, out_specs=out_specs,
                          out_shape=out_shape, scratch_shapes=scratch, compiler_params=_params(*sem), name=name)(*args)


def _proj_call(body, n_outer, n_inner, in_specs, out_specs, out_shape, scratch, name, args, rides):
    return _call(body, (n_outer, n_inner), ("parallel", "arbitrary"), in_specs, out_specs, out_shape, scratch, name,
                 args, rides)


def _cast_cols(src, n_cols, chunk):
    rows = src.shape[0]
    spec = pl.BlockSpec((chunk, n_cols), lambda i: (i, 0))
    return pl.pallas_call(_cast_kernel, grid=(rows // chunk,), in_specs=[spec], out_specs=spec,
                          out_shape=jax.ShapeDtypeStruct((rows, n_cols), BF16), compiler_params=_params("parallel"),
                          name="cast_cols")(src)


def _rmsnorm(x, g, out_dtype, tm):
    rows, d = x.shape
    return pl.pallas_call(
        _rmsnorm_kernel,
        grid=(rows // tm,),
        in_specs=[pl.BlockSpec((tm, d), lambda i: (i, 0)), pl.BlockSpec((1, d), lambda i: (0, 0))],
        out_specs=pl.BlockSpec((tm, d), lambda i: (i, 0)),
        out_shape=jax.ShapeDtypeStruct((rows, d), out_dtype),
        compiler_params=_params("parallel"),
        name="rmsnorm",
    )(x, g.reshape(1, d))


def _rmsnorm_into(x, g, buf, row0, tm):
    rows, d = x.shape
    fresh = isinstance(buf, int)
    return pl.pallas_call(
        lambda x_ref, g_ref, *rest: _rmsnorm_kernel(x_ref, g_ref, rest[-1]),
        grid=(rows // tm,),
        in_specs=[pl.BlockSpec((tm, d), lambda i: (i, 0)), pl.BlockSpec((1, d), lambda i: (0, 0))]
        + ([] if fresh else [pl.BlockSpec(memory_space=pl.ANY)]),
        out_specs=pl.BlockSpec((tm, d), lambda i: (row0 // tm + i, 0)),
        out_shape=jax.ShapeDtypeStruct((buf, d), F32) if fresh else jax.ShapeDtypeStruct(buf.shape, buf.dtype),
        input_output_aliases={} if fresh else {2: 0},
        compiler_params=_params("parallel"),
        name="rmsnorm_into",
    )(x, g.reshape(1, d), *([] if fresh else [buf]))


def _matmul_kernel(x_ref, w_ref, o_ref, *, gate):
    p = jnp.dot(x_ref[...], w_ref[...], preferred_element_type=F32)
    o_ref[...] = _sigmoid(p) if gate else p


def _matmul(x, w, col0, n, tm, tn, gate=False):
    rows, k = x.shape
    return pl.pallas_call(
        functools.partial(_matmul_kernel, gate=gate),
        grid=(rows // tm, n // tn),
        in_specs=[pl.BlockSpec((tm, k), lambda i, j: (i, 0)), pl.BlockSpec((k, tn), lambda i, j: (0, col0 // tn + j))],
        out_specs=pl.BlockSpec((tm, tn), lambda i, j: (i, j)),
        out_shape=jax.ShapeDtypeStruct((rows, n), F32),
        compiler_params=_params("parallel", "parallel"),
        name="in_proj",
    )(x, w)


def _shift_proj_kernel(x_ref, w_ref, mu_ref, m_ref, o_ref, *rest, tile, tiles_per_seq):
    i = pl.program_id(1)
    carry = rest[-1]

    @pl.when(i % tiles_per_seq == 0)
    def _():
        carry[...] = m_ref[...]

    p = jnp.dot(x_ref[...], w_ref[...], preferred_element_type=F32)
    row = lax.broadcasted_iota(jnp.int32, p.shape, 0)
    prev = jnp.where(row == 0, carry[7:8, :], pltpu.roll(p, 1, 0))
    carry[...] = p[tile - 8:tile, :]
    o_ref[...] = p + (prev - p) * mu_ref[0:1, :]
    for raw_ref in rest[:-1]:
        raw_ref[...] = p


def _shift_proj(x, w, col_blk, mu, p_before, batch, tm, tn, rides=(), raw=False):
    rows, k = x.shape
    n = mu.shape[0]
    tiles_per_seq = rows // batch // tm
    m_blk = p_before.shape[0] // 8 - 1
    mu8 = jnp.zeros((8, n), F32).at[0].set(mu)
    out = _proj_call(
        functools.partial(_shift_proj_kernel, tile=tm, tiles_per_seq=tiles_per_seq), n // tn, rows // tm,
        [pl.BlockSpec((tm, k), lambda j, i: (i, 0)), pl.BlockSpec((k, tn), lambda j, i: (0, col_blk(j))),
         pl.BlockSpec((8, tn), lambda j, i: (0, j)), pl.BlockSpec((8, tn), lambda j, i: (m_blk, j))],
        [pl.BlockSpec((tm, tn), lambda j, i: (i, j))] * (1 + raw), [jax.ShapeDtypeStruct((rows, n), F32)] * (1 + raw),
        [pltpu.VMEM((8, tn), F32)], "shift_proj", (x, w, mu8, p_before), rides)
    return out[0] if len(out) == 1 else out


_K_MU, _K_W0, _K_A0, _K_KK, _K_KA = range(5)


def _key_proj_kernel(x_ref, w_ref, par_ref, m_ref, zl_ref, w2_ref, a2_ref, k_ref, kk_ref, ka_ref, lw_ref, *rest,
                     tile, tiles_per_seq):
    i = pl.program_id(1)
    carry = rest[-1]

    @pl.when(i % tiles_per_seq == 0)
    def _():
        carry[...] = m_ref[...]

    par = par_ref[...]
    prow = lambda r: par[r:r + 1, :]
    zw = jnp.tanh(zl_ref[:, :LORA]).astype(BF16)
    za = zl_ref[:, LORA:].astype(BF16)
    xw = jnp.dot(zw, w2_ref[...], preferred_element_type=F32)
    xa = jnp.dot(za, a2_ref[...], preferred_element_type=F32)
    p = jnp.dot(x_ref[...], w_ref[...], preferred_element_type=F32)
    lw_ref[...] = LOG2_DECAY_SCALE * _sigmoid(prow(_K_W0) + xw)
    a = _sigmoid(prow(_K_A0) + xa)
    row = lax.broadcasted_iota(jnp.int32, p.shape, 0)
    prev = jnp.where(row == 0, carry[7:8, :], pltpu.roll(p, 1, 0))
    carry[...] = p[tile - 8:tile, :]
    for raw_ref in rest[:-1]:
        raw_ref[...] = p
    z = p + (prev - p) * prow(_K_MU)
    k_ref[...] = z * (1.0 + (a - 1.0) * prow(_K_KA))
    kk = z * prow(_K_KK)
    sq = (kk * kk).astype(BF16)
    ones_bd = _ones_bd()
    ss = jnp.concatenate([jnp.dot(sq[:, c:c + GL], ones_bd, preferred_element_type=F32)
                          for c in range(0, sq.shape[1], GL)], axis=1)
    kk = kk * lax.rsqrt(jnp.maximum(ss, 1e-24))
    kk_ref[...] = kk
    ka_ref[...] = kk * a


def _key_proj(x, w, col_blk, par, p_before, z_lora, w2, a2, batch, tm, tn, rides=(), raw=False):
    rows, k = x.shape
    n = par.shape[1]
    tiles_per_seq = rows // batch // tm
    m_blk = p_before.shape[0] // 8 - 1
    return _proj_call(
        functools.partial(_key_proj_kernel, tile=tm, tiles_per_seq=tiles_per_seq), n // tn, rows // tm,
        [pl.BlockSpec((tm, k), lambda j, i: (i, 0)), pl.BlockSpec((k, tn), lambda j, i: (0, col_blk(j))),
         pl.BlockSpec((8, tn), lambda j, i: (0, j)), pl.BlockSpec((8, tn), lambda j, i: (m_blk, j)),
         pl.BlockSpec((tm, 2 * LORA), lambda j, i: (i, 0)),
         pl.BlockSpec((LORA, tn), lambda j, i: (0, j)), pl.BlockSpec((LORA, tn), lambda j, i: (0, j))],
        [pl.BlockSpec((tm, tn), lambda j, i: (i, j))] * (4 + raw), [jax.ShapeDtypeStruct((rows, n), F32)] * (4 + raw),
        [pltpu.VMEM((8, tn), F32)], "key_proj", (x, w, par, p_before, z_lora, w2, a2), rides)


def _out_proj_kernel(a_ref, b_ref, wa_ref, wb_ref, h_ref, o_ref):
    acc = jnp.dot(a_ref[...], wa_ref[...], preferred_element_type=F32)
    acc += jnp.dot(b_ref[...], wb_ref[...], preferred_element_type=F32)
    o_ref[...] = h_ref[...] + acc


def _out_proj(ya, yb, w, h, row0, rows, tm, tn, rides=()):
    k = ya.shape[1]
    n = w.shape[1]
    i0 = row0 // tm
    out = _call(
        _out_proj_kernel, (rows // tm, n // tn), ("parallel", "parallel"),
        [pl.BlockSpec((tm, k), lambda i, j: (i0 + i, 0)), pl.BlockSpec((tm, k), lambda i, j: (i0 + i, 0)),
         pl.BlockSpec((k, tn), lambda i, j: (0, j)), pl.BlockSpec((k, tn), lambda i, j: (1, j)),
         pl.BlockSpec((tm, tn), lambda i, j: (i0 + i, j))],
        [pl.BlockSpec((tm, tn), lambda i, j: (i, j))], [jax.ShapeDtypeStruct((rows, n), F32)],
        [], "out_proj", (ya, yb, w, w, h), rides)
    return out[0] if len(out) == 1 else out


def _mlp_up_kernel(x_ref, w_ref, o_ref):
    hid = jnp.dot(x_ref[...], w_ref[...], preferred_element_type=F32)
    o_ref[...] = jnp.square(jnp.maximum(hid, 0.0)).astype(o_ref.dtype)


def _mlp_up(x, w, tm, tn, rides=()):
    rows, k = x.shape
    n = w.shape[1]
    out = _call(
        _mlp_up_kernel, (rows // tm, n // tn), ("parallel", "parallel"),
        [pl.BlockSpec((tm, k), lambda i, j: (i, 0)), pl.BlockSpec((k, tn), lambda i, j: (0, j))],
        [pl.BlockSpec((tm, tn), lambda i, j: (i, j))], [jax.ShapeDtypeStruct((rows, n), BF16)],
        [], "mlp_up", (x, w), rides)
    return out[0] if len(out) == 1 else out


def _mlp_down_kernel(x_ref, w_ref, h_ref, o_ref):
    @pl.when(pl.program_id(2) == 0)
    def _():
        o_ref[...] = h_ref[...]

    o_ref[...] += jnp.dot(x_ref[...], w_ref[...], preferred_element_type=F32)


def _mlp_down(x, w, h, tm, tn, tk, rides=()):
    rows, k = x.shape
    n = w.shape[1]
    out = _call(
        _mlp_down_kernel, (rows // tm, n // tn, k // tk), ("parallel", "parallel", "arbitrary"),
        [pl.BlockSpec((tm, tk), lambda i, j, q: (i, q)), pl.BlockSpec((tk, tn), lambda i, j, q: (q, j)),
         pl.BlockSpec((tm, tn), lambda i, j, q: (i, j))],
        [pl.BlockSpec((tm, tn), lambda i, j, q: (i, j))], [jax.ShapeDtypeStruct((rows, n), F32)],
        [], "mlp_down", (x, w, h), rides)
    return out[0] if len(out) == 1 else out


_P_RK, _P_LNW, _P_LNB = range(3)
_P_ROWS = 8


def _rwkv_kernel(r_ref, k_ref, v_ref, kk_ref, ka_ref, lw_ref, gate_ref, par_ref, s0_ref,
                 y_ref, sout_ref, state, y_s, *, tile, ng):
    t_idx = pl.program_id(2)

    @pl.when(t_idx == 0)
    def _():
        state[...] = s0_ref[...]

    par = par_ref[0]
    prow = lambda i: par[i:i + 1, :]
    ones_bd = _ones_bd()

    def head_sum(xb):
        return jnp.concatenate([jnp.dot(xb[:, g * GL:(g + 1) * GL], ones_bd, preferred_element_type=F32)
                                for g in range(ng)], axis=1)

    ti = lax.broadcasted_iota(jnp.int32, (CHUNK, GL), 0)
    lane = lax.broadcasted_iota(jnp.int32, (CHUNK, GL), 1)
    si = lane % HEAD
    strict = si < ti
    incl = si <= ti
    eye = jnp.where(si == ti, 1.0, 0.0).astype(F32)
    head_of_lane = [lane // HEAD == h for h in range(GROUP)]
    bd_mask = (lax.broadcasted_iota(jnp.int32, (GL, GL), 0) // HEAD
               == lax.broadcasted_iota(jnp.int32, (GL, GL), 1) // HEAD)
    tri2 = (lax.broadcasted_iota(jnp.int32, (CHUNK, 2 * CHUNK), 1) % CHUNK
            <= lax.broadcasted_iota(jnp.int32, (CHUNK, 2 * CHUNK), 0)).astype(BF16)

    def bd(x):
        xb = x.astype(BF16)
        zero = jnp.zeros((), BF16)
        return jnp.concatenate([jnp.where(m, xb, zero) for m in head_of_lane], axis=0)

    def rob_mm(x, y):
        return jnp.dot(x.astype(BF16), bd(y), preferred_element_type=F32)

    def nt_dot(x, y):
        return lax.dot_general(x, y, (((1,), (1,)), ((), ())), preferred_element_type=F32)

    def group_chunk(rows, g):
        cols = slice(g * GL, (g + 1) * GL)
        rc, kc, vc, kkc, kka, lwc = (s[rows, cols] for s in (r_ref, k_ref, v_ref, kk_ref, ka_ref, lw_ref))
        hi = lwc.astype(BF16)
        lo = (lwc - hi.astype(F32)).astype(BF16)
        cum = jnp.dot(tri2, jnp.concatenate([hi, lo], axis=0), preferred_element_type=F32)
        yield
        cum_end = cum[CHUNK - 1:CHUNK, :]
        e_cum = jnp.exp2(cum)
        e_neg = jnp.exp2(-cum)
        e_prev = jnp.exp2(cum - lwc)
        e_end = jnp.exp2(cum_end - cum)
        al = -kkc * e_prev
        rt = rc * e_cum
        bt = kka * e_neg
        kt = kc * e_neg
        bh = kka * e_end
        kh = kc * e_end
        lr = jnp.concatenate([al, rt], axis=0).astype(BF16)
        x_b = nt_dot(lr, bd(bt))
        x_k = nt_dot(lr, bd(kt))
        s_old = state[g]
        ars = nt_dot(lr, s_old.astype(BF16))
        yield
        a_ab = jnp.where(strict, x_b[:CHUNK], 0.0)
        a_rb = jnp.where(incl, x_b[CHUNK:], 0.0)
        a_ak = jnp.where(strict, x_k[:CHUNK], 0.0)
        a_rk = jnp.where(incl, x_k[CHUNK:], 0.0)
        av = ars + rob_mm(jnp.concatenate([a_ak, a_rk], axis=0), vc)
        steps = int(math.log2(CHUNK))
        pw = rob_mm(a_ab, a_ab)
        tinv = eye + a_ab
        yield
        for _ in range(steps - 2):
            both = rob_mm(jnp.concatenate([pw, tinv], axis=0), pw)
            pw = both[:CHUNK]
            tinv = tinv + both[CHUNK:]
            yield
        tinv = tinv + rob_mm(tinv, pw)
        yield
        u = rob_mm(tinv, av[:CHUNK])
        yield
        y_s[rows, cols] = av[CHUNK:] + rob_mm(a_rb, u)
        uv = jnp.concatenate([u, vc], axis=0).astype(BF16)
        bk = jnp.concatenate([bh, kh], axis=0).astype(BF16)
        ds = lax.dot_general(uv, bk, (((0,), (0,)), ((), ())), preferred_element_type=F32)
        state[g] = s_old * jnp.exp2(cum_end) + jnp.where(bd_mask, ds, 0.0)

    def chunk_body(c, _):
        rows = pl.ds(pl.multiple_of(c * CHUNK, CHUNK), CHUNK)
        live = [group_chunk(rows, g) for g in range(ng)]
        while live:
            live = [gen for gen in live if next(gen, StopIteration) is not StopIteration]
        return 0

    lax.fori_loop(0, tile // CHUNK, chunk_body, 0)

    y = y_s[...]
    inv_n = 1.0 / HEAD
    y_hi = y.astype(BF16)
    y_lo = (y - y_hi.astype(F32)).astype(BF16)
    d = y - (head_sum(y_hi) + head_sum(y_lo)) * inv_n
    var = head_sum((d * d).astype(BF16)) * inv_n
    yn = d * lax.rsqrt(var + GN_EPS) * prow(_P_LNW) + prow(_P_LNB)
    bonus = head_sum((r_ref[...] * k_ref[...] * prow(_P_RK)).astype(BF16)) * v_ref[...]
    y_ref[...] = (gate_ref[...] * (yn + bonus)).astype(y_ref.dtype)

    @pl.when(t_idx == pl.num_programs(2) - 1)
    def _():
        sout_ref[0] = state[...]


def _rwkv(z_rv, keyed, gate, par, s0, batch, tile, ng):
    rows = z_rv.shape[0]
    d = z_rv.shape[1] // 2
    groups = d // GL
    sg = groups // ng
    width = ng * GL
    nt = rows // batch // tile
    wide = lambda arr: arr.reshape(sg, ng, arr.shape[1], GL).transpose(0, 2, 1, 3).reshape(sg, arr.shape[1], width)
    row_blk = lambda off: pl.BlockSpec((tile, width), lambda b, g, t: (b * nt + t, off + g))
    k, kk, ka, lw = keyed
    return pl.pallas_call(
        functools.partial(_rwkv_kernel, tile=tile, ng=ng),
        grid=(batch, sg, nt),
        in_specs=[row_blk(0), row_blk(0), row_blk(sg), row_blk(0), row_blk(0), row_blk(0), row_blk(0),
                  pl.BlockSpec((1, _P_ROWS, width), lambda b, g, t: (g, 0, 0)),
                  pl.BlockSpec((ng, GL, GL), lambda b, g, t: (g, 0, 0))],
        out_specs=[row_blk(0), pl.BlockSpec((1, ng, GL, GL), lambda b, g, t: (b * sg + g, 0, 0, 0))],
        out_shape=[jax.ShapeDtypeStruct((rows, d), BF16), jax.ShapeDtypeStruct((batch * sg, ng, GL, GL), F32)],
        scratch_shapes=[pltpu.VMEM((ng, GL, GL), F32), pltpu.VMEM((tile, width), F32)],
        compiler_params=_params("parallel", "parallel", "arbitrary"),
        name="rwkv7",
    )(z_rv, k, z_rv, kk, ka, lw, gate, wide(par), s0)


def _conv_proj_kernel(u_ref, wb_ref, wc_ref, wh_ref, wg_ref, cw_ref, mc_ref, mh_ref, o_ref, carry,
                      *, tile, tiles_per_seq):
    i = pl.program_id(1)

    @pl.when(i % tiles_per_seq == 0)
    def _():
        carry[...] = mc_ref[...] * mh_ref[...]

    x = u_ref[...]
    proj = lambda w_ref: jnp.dot(x, w_ref[...], preferred_element_type=F32)
    u = proj(wc_ref) * proj(wh_ref)
    row = lax.broadcasted_iota(jnp.int32, u.shape, 0)
    prev1 = jnp.where(row == 0, carry[7:8, :], pltpu.roll(u, 1, 0))
    prev2 = jnp.where(row == 0, carry[6:7, :], jnp.where(row == 1, carry[7:8, :], pltpu.roll(u, 2, 0)))
    carry[...] = u[tile - 8:tile, :]
    w = cw_ref[...]
    conv = w[0:1, :] * prev2 + w[1:2, :] * prev1 + w[2:3, :] * u
    o_ref[...] = (_sigmoid(proj(wg_ref)) * (proj(wb_ref) * conv)).astype(o_ref.dtype)


def _conv_proj(u, w, col0, gate_col0, conv_w, p_conv_meta, batch, tile, tc, rides=()):
    rows, k = u.shape
    d = conv_w.shape[1]
    nc = d // tc
    tiles_per_seq = rows // batch // tile
    meta_blk = p_conv_meta.shape[0] // 8 - 1
    wblk = lambda col: pl.BlockSpec((k, tc), lambda j, i: (0, col // tc + j))
    w8 = jnp.zeros((8, d), F32).at[:conv_w.shape[0]].set(conv_w)
    out = _proj_call(
        functools.partial(_conv_proj_kernel, tile=tile, tiles_per_seq=tiles_per_seq), nc, rows // tile,
        [pl.BlockSpec((tile, k), lambda j, i: (i, 0)),
         wblk(col0), wblk(col0 + d), wblk(col0 + 2 * d), wblk(gate_col0),
         pl.BlockSpec((8, tc), lambda j, i: (0, j)),
         pl.BlockSpec((8, tc), lambda j, i: (meta_blk, nc + j)),
         pl.BlockSpec((8, tc), lambda j, i: (meta_blk, 2 * nc + j))],
        [pl.BlockSpec((tile, tc), lambda j, i: (i, j))], [jax.ShapeDtypeStruct((rows, d), BF16)],
        [pltpu.VMEM((8, tc), F32)], "conv_proj", (u, w, w, w, w, w8, p_conv_meta, p_conv_meta), rides)
    return out[0] if len(out) == 1 else out


def _pick(n, prefs):
    for p in prefs:
        if n % p == 0:
            return p
    raise ValueError(f"no tile for {n} in {prefs}")


def kernel(x, meta_tokens, norm_mix_g, w_in, rwkv_shift_mu, rwkv_w0, rwkv_w2, rwkv_a0, rwkv_a2, rwkv_k_k, rwkv_k_a, rwkv_r_k, rwkv_ln_w, rwkv_ln_b, conv_w, w_out, norm_mlp_g, w_up, w_down, norm_final_g):
    bsz, seq, d = x.shape
    n_meta = meta_tokens.shape[0]
    depth = w_in.shape[0]
    assert depth == 1, "the meta-token hand-off is written for a single layer"
    assert d % GL == 0 and seq % CHUNK == 0 and n_meta <= META_ROWS
    groups = d // GL
    rows = bsz * seq
    layer = 0

    c_rkv, c_lora = 3 * d, 2 * LORA
    c0 = c_rkv + c_lora
    n_all = w_in.shape[2]
    tc = _pick(d, (256, 128))
    assert c_rkv % c_lora == 0 and n_all == c0 + 5 * d, "unexpected column layout of w_in"
    w_a = _cast_cols(w_in[layer], c0, _pick(d, (128, 64, 32, 16)))
    w2 = rwkv_w2[layer].astype(BF16)
    a2 = rwkv_a2[layer].astype(BF16)

    mu = rwkv_shift_mu[layer]
    mu_rv = jnp.concatenate([mu[:d], mu[2 * d:c_rkv]])
    rows8 = lambda vecs: jnp.zeros((8, d), F32).at[:len(vecs)].set(jnp.stack(vecs))
    key_par = rows8([mu[d:2 * d], rwkv_w0[layer], rwkv_a0[layer], rwkv_k_k[layer], rwkv_k_a[layer]])
    per_group = lambda vec: vec.reshape(groups, 1, GL)
    par = jnp.concatenate(
        [per_group(rwkv_r_k[layer].reshape(d)), per_group(rwkv_ln_w[layer]), per_group(rwkv_ln_b[layer]),
         jnp.zeros((groups, _P_ROWS - 3, GL), F32)], axis=1)
    tw = _pick(d, COL_TILES)
    tk = _pick(d, (512, 256))
    rv_blk = lambda j: j + (j // (d // tw)) * (d // tw)
    k_blk = lambda j: d // tk + j
    lora_blk = lambda j: c_rkv // c_lora + j

    def mixer_inputs(h_rows, batch, tm, before, casts=((), ()), raw=False):
        u = _rmsnorm(h_rows, norm_mix_g[layer], BF16, min(tm, 512))
        z_lora = _shift_proj(u, w_a, lora_blk, mu[c_rkv:c0], before[0], batch, tm, c_lora, raw=raw)
        z_rv = _shift_proj(u, w_a, rv_blk, mu_rv, before[1], batch, tm, tw, casts[0], raw)
        keyed = _key_proj(u, w_a, k_blk, key_par, before[2], z_lora[0] if raw else z_lora, w2, a2, batch, tm, tk,
                          casts[1], raw)
        return u, z_lora, z_rv, keyed

    meta = jnp.zeros((META_ROWS, d), F32).at[META_ROWS - n_meta:].set(meta_tokens.astype(F32))
    nothing = (jnp.zeros((8, c_lora), F32), jnp.zeros((8, 2 * d), F32), jnp.zeros((8, d), F32))
    u_meta, (_, m_lora), (mz_rv, m_rv), m_keyed = mixer_inputs(meta, 1, META_ROWS, nothing, raw=True)
    ng = _pick(groups, (8, 4, 2, 1))
    _, s_meta = _rwkv(mz_rv, m_keyed[:4], jnp.zeros((META_ROWS, d), F32), par, jnp.zeros((groups, GL, GL), F32),
                      1, META_ROWS, ng)

    xf = x.reshape(rows, d)
    tm = _pick(rows, ROW_TILES)
    t_seq = _pick(seq, ROW_TILES)
    n_row = rows // t_seq
    assert bsz % 2 == 0, "the tail of the block alternates between two halves of the batch"
    half = rows // 2
    th = _pick(half, ROW_TILES)
    tn_out = _pick(d, COL_TILES[2:])
    ride = lambda job, steps: [job] if _ride_ok(job["src"], steps) else []
    casts = (ride(_ride_cast(w_in[layer], c0, n_all - c0), 2 * d // tw * n_row),
             ride(_ride_cast(w_out[layer]), d // tk * n_row), ride(_ride_cast(w_up[layer]), d // tc * n_row),
             ride(_ride_cast(w_down[layer]), half // th * (d // tn_out)))
    u, _, z_rv, keyed = mixer_inputs(xf, bsz, t_seq, (m_lora, m_rv, m_keyed[4]), casts[:2])
    z_rv, w_b = z_rv if casts[0] else (z_rv, w_in[layer, :, c0:].astype(BF16))
    keyed, wo = (keyed[:4], keyed[4]) if casts[1] else (keyed, w_out[layer].astype(BF16))
    gate = _matmul(u, w_b, 3 * d, d, tm, _pick(d, COL_TILES), gate=True)
    t_scan = _pick(seq, ROW_TILES[2:])
    ya, _ = _rwkv(z_rv, keyed, gate, par, s_meta.reshape(groups, GL, GL), bsz, t_scan, ng)
    m_conv = _shift_proj(u_meta, w_b, lambda j: j, jnp.zeros((3 * d,), F32), jnp.zeros((8, 3 * d), F32),
                         1, META_ROWS, tc)
    yb = _conv_proj(u, w_b, 0, 4 * d, conv_w[layer], m_conv, bsz, t_seq, tc, casts[2])
    yb, wu = yb if casts[2] else (yb, w_up[layer].astype(BF16))
    g_mlp, g_out = norm_mlp_g[layer], norm_final_g
    tn_up, tk_down = _pick(wu.shape[1], COL_TILES), _pick(wu.shape[1], (4096, 2048) + COL_TILES)
    h1_a = _out_proj(ya, yb, wo, xf, 0, half, th, tn_out, casts[3])
    h1_a, wd = h1_a if casts[3] else (h1_a, w_down[layer].astype(BF16))
    h1_b, u2_a = _out_proj(ya, yb, wo, xf, half, half, th, tn_out, [_ride_norm(h1_a, g_mlp, BF16)])
    hid_a, u2_b = _mlp_up(u2_a, wu, th, tn_up, [_ride_norm(h1_b, g_mlp, BF16)])
    hid_b = _mlp_up(u2_b, wu, th, tn_up)
    h2_a = _mlp_down(hid_a, wd, h1_a, th, _pick(d, COL_TILES), tk_down)
    h2_b = _mlp_down(hid_b, wd, h1_b, th, _pick(d, COL_TILES), tk_down)
    tq = _pick(half, ROW_TILES[1:])
    out = _rmsnorm_into(h2_b, g_out, _rmsnorm_into(h2_a, g_out, rows, 0, tq), half, tq)
    return out.reshape(bsz, seq, d)
```

```python
import functools
import math

import jax
import jax.numpy as jnp
from jax import lax
from jax.experimental import pallas as pl
from jax.experimental.pallas import tpu as pltpu

F32 = jnp.float32
BF16 = jnp.bfloat16

HEAD = 64
GROUP = 4
GL = GROUP * HEAD
CHUNK = 64
LORA = 128
NORM_EPS = 1e-6
GN_EPS = 64e-5
LOG2_DECAY_SCALE = -math.exp(-0.5) / math.log(2.0)
META_ROWS = 64
VMEM_LIMIT = 56 * 1024 * 1024
ROW_TILES = (1024, 512, 256, 128, 64)
COL_TILES = (1024, 512, 256, 128)


def _params(*sem):
    return pltpu.CompilerParams(dimension_semantics=sem, vmem_limit_bytes=VMEM_LIMIT)


def _sigmoid(x):
    return 0.5 * jnp.tanh(0.5 * x) + 0.5


def _ones_bd():
    return (lax.broadcasted_iota(jnp.int32, (GL, GL), 0) // HEAD
            == lax.broadcasted_iota(jnp.int32, (GL, GL), 1) // HEAD).astype(BF16)


def _cast_kernel(x_ref, o_ref):
    o_ref[...] = x_ref[...].astype(o_ref.dtype)


def _rmsnorm_kernel(x_ref, g_ref, o_ref):
    x = x_ref[...]
    y = x * lax.rsqrt(jnp.mean(x * x, axis=-1, keepdims=True) + NORM_EPS)
    o_ref[...] = (y * g_ref[...]).astype(o_ref.dtype)


def _ride_cast(src, col0=0, n_cols=None):
    return dict(src=src, col0=col0, n_cols=n_cols or src.shape[1])


def _ride_norm(src, g, dtype):
    return dict(src=src, g=g, dtype=dtype)


def _ride_ok(src, steps):
    return src.shape[0] % steps == 0 and src.shape[0] // steps % 16 == 0


def _call(body, grid, sem, in_specs, out_specs, out_shape, scratch, name, args, rides=()):
    n_in, n_out = len(in_specs), len(out_specs)
    in_specs, out_specs, out_shape, args = list(in_specs), list(out_specs), list(out_shape), list(args)

    def step(*ids):
        flat = ids[0]
        for n, i in zip(grid[1:], ids[1:]):
            flat = flat * n + i
        return flat

    jobs = []
    for job in rides:
        src = job["src"]
        assert _ride_ok(src, math.prod(grid)), (name, src.shape, grid)
        chunk = src.shape[0] // math.prod(grid)
        if "g" in job:
            d = src.shape[1]
            in_specs += [pl.BlockSpec((chunk, d), lambda *ids: (step(*ids), 0)),
                         pl.BlockSpec((1, d), lambda *ids: (0, 0))]
            args += [src, job["g"].reshape(1, d)]
            out_specs.append(pl.BlockSpec((chunk, d), lambda *ids: (step(*ids), 0)))
            out_shape.append(jax.ShapeDtypeStruct(src.shape, job["dtype"]))
            jobs.append((2, _rmsnorm_kernel))
        else:
            col0, n_cols = job["col0"], job["n_cols"]
            in_specs.append(pl.BlockSpec((pl.Element(chunk), pl.Element(n_cols)),
                                         lambda *ids, chunk=chunk, col0=col0: (step(*ids) * chunk, col0)))
            args.append(src)
            out_specs.append(pl.BlockSpec((chunk, n_cols), lambda *ids: (step(*ids), 0)))
            out_shape.append(jax.ShapeDtypeStruct((src.shape[0], n_cols), BF16))
            jobs.append((1, _cast_kernel))

    def kernel(*refs):
        k = n_in
        ride_ins = []
        for n_refs, _ in jobs:
            ride_ins.append(refs[k:k + n_refs])
            k += n_refs
        ride_outs = refs[k + n_out:k + n_out + len(jobs)]
        for (_, fn), ins, out in zip(jobs, ride_ins, ride_outs):
            fn(*ins, out)
        body(*refs[:n_in], *refs[k:k + n_out], *refs[k + n_out + len(jobs):])

    return pl.pallas_call(kernel if jobs else body, grid=grid, in_specs=in_specs, out_specs=out_specs,
                          out_shape=out_shape, scratch_shapes=scratch, compiler_params=_params(*sem), name=name)(*args)


def _proj_call(body, n_outer, n_inner, in_specs, out_specs, out_shape, scratch, name, args, rides):
    return _call(body, (n_outer, n_inner), ("parallel", "arbitrary"), in_specs, out_specs, out_shape, scratch, name,
                 args, rides)


def _cast_cols(src, n_cols, chunk):
    rows = src.shape[0]
    spec = pl.BlockSpec((chunk, n_cols), lambda i: (i, 0))
    return pl.pallas_call(_cast_kernel, grid=(rows // chunk,), in_specs=[spec], out_specs=spec,
                          out_shape=jax.ShapeDtypeStruct((rows, n_cols), BF16), compiler_params=_params("parallel"),
                          name="cast_cols")(src)


def _rmsnorm(x, g, out_dtype, tm):
    rows, d = x.shape
    return pl.pallas_call(
        _rmsnorm_kernel,
        grid=(rows // tm,),
        in_specs=[pl.BlockSpec((tm, d), lambda i: (i, 0)), pl.BlockSpec((1, d), lambda i: (0, 0))],
        out_specs=pl.BlockSpec((tm, d), lambda i: (i, 0)),
        out_shape=jax.ShapeDtypeStruct((rows, d), out_dtype),
        compiler_params=_params("parallel"),
        name="rmsnorm",
    )(x, g.reshape(1, d))


def _rmsnorm_into(x, g, buf, row0, tm):
    rows, d = x.shape
    fresh = isinstance(buf, int)
    return pl.pallas_call(
        lambda x_ref, g_ref, *rest: _rmsnorm_kernel(x_ref, g_ref, rest[-1]),
        grid=(rows // tm,),
        in_specs=[pl.BlockSpec((tm, d), lambda i: (i, 0)), pl.BlockSpec((1, d), lambda i: (0, 0))]
        + ([] if fresh else [pl.BlockSpec(memory_space=pl.ANY)]),
        out_specs=pl.BlockSpec((tm, d), lambda i: (row0 // tm + i, 0)),
        out_shape=jax.ShapeDtypeStruct((buf, d), F32) if fresh else jax.ShapeDtypeStruct(buf.shape, buf.dtype),
        input_output_aliases={} if fresh else {2: 0},
        compiler_params=_params("parallel"),
        name="rmsnorm_into",
    )(x, g.reshape(1, d), *([] if fresh else [buf]))


def _matmul_kernel(x_ref, w_ref, o_ref, *, gate):
    p = jnp.dot(x_ref[...], w_ref[...], preferred_element_type=F32)
    o_ref[...] = _sigmoid(p) if gate else p


def _matmul(x, w, col0, n, tm, tn, gate=False):
    rows, k = x.shape
    return pl.pallas_call(
        functools.partial(_matmul_kernel, gate=gate),
        grid=(rows // tm, n // tn),
        in_specs=[pl.BlockSpec((tm, k), lambda i, j: (i, 0)), pl.BlockSpec((k, tn), lambda i, j: (0, col0 // tn + j))],
        out_specs=pl.BlockSpec((tm, tn), lambda i, j: (i, j)),
        out_shape=jax.ShapeDtypeStruct((rows, n), F32),
        compiler_params=_params("parallel", "parallel"),
        name="in_proj",
    )(x, w)


def _shift_proj_kernel(x_ref, w_ref, mu_ref, m_ref, o_ref, *rest, tile, tiles_per_seq):
    i = pl.program_id(1)
    carry = rest[-1]

    @pl.when(i % tiles_per_seq == 0)
    def _():
        carry[...] = m_ref[...]

    p = jnp.dot(x_ref[...], w_ref[...], preferred_element_type=F32)
    row = lax.broadcasted_iota(jnp.int32, p.shape, 0)
    prev = jnp.where(row == 0, carry[7:8, :], pltpu.roll(p, 1, 0))
    carry[...] = p[tile - 8:tile, :]
    o_ref[...] = p + (prev - p) * mu_ref[0:1, :]
    for raw_ref in rest[:-1]:
        raw_ref[...] = p


def _shift_proj(x, w, col_blk, mu, p_before, batch, tm, tn, rides=(), raw=False):
    rows, k = x.shape
    n = mu.shape[0]
    tiles_per_seq = rows // batch // tm
    m_blk = p_before.shape[0] // 8 - 1
    mu8 = jnp.zeros((8, n), F32).at[0].set(mu)
    out = _proj_call(
        functools.partial(_shift_proj_kernel, tile=tm, tiles_per_seq=tiles_per_seq), n // tn, rows // tm,
        [pl.BlockSpec((tm, k), lambda j, i: (i, 0)), pl.BlockSpec((k, tn), lambda j, i: (0, col_blk(j))),
         pl.BlockSpec((8, tn), lambda j, i: (0, j)), pl.BlockSpec((8, tn), lambda j, i: (m_blk, j))],
        [pl.BlockSpec((tm, tn), lambda j, i: (i, j))] * (1 + raw), [jax.ShapeDtypeStruct((rows, n), F32)] * (1 + raw),
        [pltpu.VMEM((8, tn), F32)], "shift_proj", (x, w, mu8, p_before), rides)
    return out[0] if len(out) == 1 else out


_K_MU, _K_W0, _K_A0, _K_KK, _K_KA = range(5)


def _key_proj_kernel(x_ref, w_ref, par_ref, m_ref, zl_ref, w2_ref, a2_ref, k_ref, kk_ref, ka_ref, lw_ref, *rest,
                     tile, tiles_per_seq):
    i = pl.program_id(1)
    carry = rest[-1]

    @pl.when(i % tiles_per_seq == 0)
    def _():
        carry[...] = m_ref[...]

    par = par_ref[...]
    prow = lambda r: par[r:r + 1, :]
    zw = jnp.tanh(zl_ref[:, :LORA]).astype(BF16)
    za = zl_ref[:, LORA:].astype(BF16)
    xw = jnp.dot(zw, w2_ref[...], preferred_element_type=F32)
    xa = jnp.dot(za, a2_ref[...], preferred_element_type=F32)
    p = jnp.dot(x_ref[...], w_ref[...], preferred_element_type=F32)
    lw_ref[...] = LOG2_DECAY_SCALE * _sigmoid(prow(_K_W0) + xw)
    a = _sigmoid(prow(_K_A0) + xa)
    row = lax.broadcasted_iota(jnp.int32, p.shape, 0)
    prev = jnp.where(row == 0, carry[7:8, :], pltpu.roll(p, 1, 0))
    carry[...] = p[tile - 8:tile, :]
    for raw_ref in rest[:-1]:
        raw_ref[...] = p
    z = p + (prev - p) * prow(_K_MU)
    k_ref[...] = z * (1.0 + (a - 1.0) * prow(_K_KA))
    kk = z * prow(_K_KK)
    sq = (kk * kk).astype(BF16)
    ones_bd = _ones_bd()
    ss = jnp.concatenate([jnp.dot(sq[:, c:c + GL], ones_bd, preferred_element_type=F32)
                          for c in range(0, sq.shape[1], GL)], axis=1)
    kk = kk * lax.rsqrt(jnp.maximum(ss, 1e-24))
    kk_ref[...] = kk
    ka_ref[...] = kk * a


def _key_proj(x, w, col_blk, par, p_before, z_lora, w2, a2, batch, tm, tn, rides=(), raw=False):
    rows, k = x.shape
    n = par.shape[1]
    tiles_per_seq = rows // batch // tm
    m_blk = p_before.shape[0] // 8 - 1
    return _proj_call(
        functools.partial(_key_proj_kernel, tile=tm, tiles_per_seq=tiles_per_seq), n // tn, rows // tm,
        [pl.BlockSpec((tm, k), lambda j, i: (i, 0)), pl.BlockSpec((k, tn), lambda j, i: (0, col_blk(j))),
         pl.BlockSpec((8, tn), lambda j, i: (0, j)), pl.BlockSpec((8, tn), lambda j, i: (m_blk, j)),
         pl.BlockSpec((tm, 2 * LORA), lambda j, i: (i, 0)),
         pl.BlockSpec((LORA, tn), lambda j, i: (0, j)), pl.BlockSpec((LORA, tn), lambda j, i: (0, j))],
        [pl.BlockSpec((tm, tn), lambda j, i: (i, j))] * (4 + raw), [jax.ShapeDtypeStruct((rows, n), F32)] * (4 + raw),
        [pltpu.VMEM((8, tn), F32)], "key_proj", (x, w, par, p_before, z_lora, w2, a2), rides)


def _out_proj_kernel(a_ref, b_ref, wa_ref, wb_ref, h_ref, o_ref):
    acc = jnp.dot(a_ref[...], wa_ref[...], preferred_element_type=F32)
    acc += jnp.dot(b_ref[...], wb_ref[...], preferred_element_type=F32)
    o_ref[...] = h_ref[...] + acc


def _out_proj(ya, yb, w, h, row0, rows, tm, tn, rides=()):
    k = ya.shape[1]
    n = w.shape[1]
    i0 = row0 // tm
    out = _call(
        _out_proj_kernel, (rows // tm, n // tn), ("parallel", "parallel"),
        [pl.BlockSpec((tm, k), lambda i, j: (i0 + i, 0)), pl.BlockSpec((tm, k), lambda i, j: (i0 + i, 0)),
         pl.BlockSpec((k, tn), lambda i, j: (0, j)), pl.BlockSpec((k, tn), lambda i, j: (1, j)),
         pl.BlockSpec((tm, tn), lambda i, j: (i0 + i, j))],
        [pl.BlockSpec((tm, tn), lambda i, j: (i, j))], [jax.ShapeDtypeStruct((rows, n), F32)],
        [], "out_proj", (ya, yb, w, w, h), rides)
    return out[0] if len(out) == 1 else out


def _mlp_up_kernel(x_ref, w_ref, o_ref):
    hid = jnp.dot(x_ref[...], w_ref[...], preferred_element_type=F32)
    o_ref[...] = jnp.square(jnp.maximum(hid, 0.0)).astype(o_ref.dtype)


def _mlp_up(x, w, tm, tn, rides=()):
    rows, k = x.shape
    n = w.shape[1]
    out = _call(
        _mlp_up_kernel, (rows // tm, n // tn), ("parallel", "parallel"),
        [pl.BlockSpec((tm, k), lambda i, j: (i, 0)), pl.BlockSpec((k, tn), lambda i, j: (0, j))],
        [pl.BlockSpec((tm, tn), lambda i, j: (i, j))], [jax.ShapeDtypeStruct((rows, n), BF16)],
        [], "mlp_up", (x, w), rides)
    return out[0] if len(out) == 1 else out


def _mlp_down_kernel(x_ref, w_ref, h_ref, o_ref):
    @pl.when(pl.program_id(2) == 0)
    def _():
        o_ref[...] = h_ref[...]

    o_ref[...] += jnp.dot(x_ref[...], w_ref[...], preferred_element_type=F32)


def _mlp_down(x, w, h, tm, tn, tk, rides=()):
    rows, k = x.shape
    n = w.shape[1]
    out = _call(
        _mlp_down_kernel, (rows // tm, n // tn, k // tk), ("parallel", "parallel", "arbitrary"),
        [pl.BlockSpec((tm, tk), lambda i, j, q: (i, q)), pl.BlockSpec((tk, tn), lambda i, j, q: (q, j)),
         pl.BlockSpec((tm, tn), lambda i, j, q: (i, j))],
        [pl.BlockSpec((tm, tn), lambda i, j, q: (i, j))], [jax.ShapeDtypeStruct((rows, n), F32)],
        [], "mlp_down", (x, w, h), rides)
    return out[0] if len(out) == 1 else out


_P_RK, _P_LNW, _P_LNB = range(3)
_P_ROWS = 8


def _rwkv_kernel(r_ref, k_ref, v_ref, kk_ref, ka_ref, lw_ref, gate_ref, par_ref, s0_ref,
                 y_ref, sout_ref, state, y_s, *, tile, ng):
    t_idx = pl.program_id(2)

    @pl.when(t_idx == 0)
    def _():
        state[...] = s0_ref[...]

    par = par_ref[0]
    prow = lambda i: par[i:i + 1, :]
    ones_bd = _ones_bd()

    def head_sum(xb):
        return jnp.concatenate([jnp.dot(xb[:, g * GL:(g + 1) * GL], ones_bd, preferred_element_type=F32)
                                for g in range(ng)], axis=1)

    ti = lax.broadcasted_iota(jnp.int32, (CHUNK, GL), 0)
    lane = lax.broadcasted_iota(jnp.int32, (CHUNK, GL), 1)
    si = lane % HEAD
    strict = si < ti
    incl = si <= ti
    eye = jnp.where(si == ti, 1.0, 0.0).astype(F32)
    head_of_lane = [lane // HEAD == h for h in range(GROUP)]
    bd_mask = (lax.broadcasted_iota(jnp.int32, (GL, GL), 0) // HEAD
               == lax.broadcasted_iota(jnp.int32, (GL, GL), 1) // HEAD)
    tri2 = (lax.broadcasted_iota(jnp.int32, (CHUNK, 2 * CHUNK), 1) % CHUNK
            <= lax.broadcasted_iota(jnp.int32, (CHUNK, 2 * CHUNK), 0)).astype(BF16)

    def bd(x):
        xb = x.astype(BF16)
        zero = jnp.zeros((), BF16)
        return jnp.concatenate([jnp.where(m, xb, zero) for m in head_of_lane], axis=0)

    def rob_mm(x, y):
        return jnp.dot(x.astype(BF16), bd(y), preferred_element_type=F32)

    def nt_dot(x, y):
        return lax.dot_general(x, y, (((1,), (1,)), ((), ())), preferred_element_type=F32)

    def scan_prepare(rows, g, out):
        cols = slice(g * GL, (g + 1) * GL)
        rc, kc, vc, kkc, kka, lwc = (s[rows, cols] for s in (r_ref, k_ref, v_ref, kk_ref, ka_ref, lw_ref))
        hi = lwc.astype(BF16)
        lo = (lwc - hi.astype(F32)).astype(BF16)
        cum = jnp.dot(tri2, jnp.concatenate([hi, lo], axis=0), preferred_element_type=F32)
        yield
        cum_end = cum[CHUNK - 1:CHUNK, :]
        e_cum = jnp.exp2(cum)
        e_neg = jnp.exp2(-cum)
        e_prev = jnp.exp2(cum - lwc)
        e_end = jnp.exp2(cum_end - cum)
        al = -kkc * e_prev
        rt = rc * e_cum
        bt = kka * e_neg
        kt = kc * e_neg
        bh = kka * e_end
        kh = kc * e_end
        lr = jnp.concatenate([al, rt], axis=0).astype(BF16)
        x_b = nt_dot(lr, bd(bt))
        x_k = nt_dot(lr, bd(kt))
        yield
        a_ab = jnp.where(strict, x_b[:CHUNK], 0.0)
        a_rb = jnp.where(incl, x_b[CHUNK:], 0.0)
        a_ak = jnp.where(strict, x_k[:CHUNK], 0.0)
        a_rk = jnp.where(incl, x_k[CHUNK:], 0.0)
        av = rob_mm(jnp.concatenate([a_ak, a_rk], axis=0), vc)
        steps = int(math.log2(CHUNK))
        pw = rob_mm(a_ab, a_ab)
        tinv = eye + a_ab
        yield
        for _ in range(steps - 2):
            both = rob_mm(jnp.concatenate([pw, tinv], axis=0), pw)
            pw = both[:CHUNK]
            tinv = tinv + both[CHUNK:]
            yield
        tinv = tinv + rob_mm(tinv, pw)
        out[g] = dict(lr=lr, av=av, tinv=tinv.astype(BF16), a_rb=a_rb.astype(BF16), v=vc.astype(BF16),
                      bk=jnp.concatenate([bh, kh], axis=0).astype(BF16), decay=jnp.exp2(cum_end))

    def scan_apply(rows, g, p):
        cols = slice(g * GL, (g + 1) * GL)
        s_old = state[g]
        av = p["av"] + nt_dot(p["lr"], s_old.astype(BF16))
        yield
        u = jnp.dot(p["tinv"], bd(av[:CHUNK]), preferred_element_type=F32)
        yield
        y_s[rows, cols] = av[CHUNK:] + jnp.dot(p["a_rb"], bd(u), preferred_element_type=F32)
        uv = jnp.concatenate([u.astype(BF16), p["v"]], axis=0)
        ds = lax.dot_general(uv, p["bk"], (((0,), (0,)), ((), ())), preferred_element_type=F32)
        state[g] = s_old * p["decay"] + jnp.where(bd_mask, ds, 0.0)

    def round_robin(gens):
        while gens:
            gens = [gen for gen in gens if next(gen, StopIteration) is not StopIteration]

    n_chunks = tile // CHUNK
    span = 2 if n_chunks % 2 == 0 else 1

    def chunk_body(c, _):
        rows = [pl.ds(pl.multiple_of((c * span + i) * CHUNK, CHUNK), CHUNK) for i in range(span)]
        prepared = [[None] * ng for _ in range(span)]
        round_robin([scan_prepare(rows[i], g, prepared[i]) for i in range(span) for g in range(ng)])
        for i in range(span):
            round_robin([scan_apply(rows[i], g, prepared[i][g]) for g in range(ng)])
        return 0

    lax.fori_loop(0, n_chunks // span, chunk_body, 0)

    y = y_s[...]
    inv_n = 1.0 / HEAD
    y_hi = y.astype(BF16)
    y_lo = (y - y_hi.astype(F32)).astype(BF16)
    d = y - (head_sum(y_hi) + head_sum(y_lo)) * inv_n
    var = head_sum((d * d).astype(BF16)) * inv_n
    yn = d * lax.rsqrt(var + GN_EPS) * prow(_P_LNW) + prow(_P_LNB)
    bonus = head_sum((r_ref[...] * k_ref[...] * prow(_P_RK)).astype(BF16)) * v_ref[...]
    y_ref[...] = (gate_ref[...] * (yn + bonus)).astype(y_ref.dtype)

    @pl.when(t_idx == pl.num_programs(2) - 1)
    def _():
        sout_ref[0] = state[...]


def _rwkv(z_rv, keyed, gate, par, s0, batch, tile, ng):
    rows = z_rv.shape[0]
    d = z_rv.shape[1] // 2
    groups = d // GL
    sg = groups // ng
    width = ng * GL
    nt = rows // batch // tile
    wide = lambda arr: arr.reshape(sg, ng, arr.shape[1], GL).transpose(0, 2, 1, 3).reshape(sg, arr.shape[1], width)
    row_blk = lambda off: pl.BlockSpec((tile, width), lambda b, g, t: (b * nt + t, off + g))
    k, kk, ka, lw = keyed
    return pl.pallas_call(
        functools.partial(_rwkv_kernel, tile=tile, ng=ng),
        grid=(batch, sg, nt),
        in_specs=[row_blk(0), row_blk(0), row_blk(sg), row_blk(0), row_blk(0), row_blk(0), row_blk(0),
                  pl.BlockSpec((1, _P_ROWS, width), lambda b, g, t: (g, 0, 0)),
                  pl.BlockSpec((ng, GL, GL), lambda b, g, t: (g, 0, 0))],
        out_specs=[row_blk(0), pl.BlockSpec((1, ng, GL, GL), lambda b, g, t: (b * sg + g, 0, 0, 0))],
        out_shape=[jax.ShapeDtypeStruct((rows, d), BF16), jax.ShapeDtypeStruct((batch * sg, ng, GL, GL), F32)],
        scratch_shapes=[pltpu.VMEM((ng, GL, GL), F32), pltpu.VMEM((tile, width), F32)],
        compiler_params=_params("parallel", "parallel", "arbitrary"),
        name="rwkv7",
    )(z_rv, k, z_rv, kk, ka, lw, gate, wide(par), s0)


def _conv_proj_kernel(u_ref, wb_ref, wc_ref, wh_ref, wg_ref, cw_ref, mc_ref, mh_ref, o_ref, carry,
                      *, tile, tiles_per_seq):
    i = pl.program_id(1)

    @pl.when(i % tiles_per_seq == 0)
    def _():
        carry[...] = mc_ref[...] * mh_ref[...]

    x = u_ref[...]
    proj = lambda w_ref: jnp.dot(x, w_ref[...], preferred_element_type=F32)
    u = proj(wc_ref) * proj(wh_ref)
    row = lax.broadcasted_iota(jnp.int32, u.shape, 0)
    prev1 = jnp.where(row == 0, carry[7:8, :], pltpu.roll(u, 1, 0))
    prev2 = jnp.where(row == 0, carry[6:7, :], jnp.where(row == 1, carry[7:8, :], pltpu.roll(u, 2, 0)))
    carry[...] = u[tile - 8:tile, :]
    w = cw_ref[...]
    conv = w[0:1, :] * prev2 + w[1:2, :] * prev1 + w[2:3, :] * u
    o_ref[...] = (_sigmoid(proj(wg_ref)) * (proj(wb_ref) * conv)).astype(o_ref.dtype)


def _conv_proj(u, w, col0, gate_col0, conv_w, p_conv_meta, batch, tile, tc, rides=()):
    rows, k = u.shape
    d = conv_w.shape[1]
    nc = d // tc
    tiles_per_seq = rows // batch // tile
    meta_blk = p_conv_meta.shape[0] // 8 - 1
    wblk = lambda col: pl.BlockSpec((k, tc), lambda j, i: (0, col // tc + j))
    w8 = jnp.zeros((8, d), F32).at[:conv_w.shape[0]].set(conv_w)
    out = _proj_call(
        functools.partial(_conv_proj_kernel, tile=tile, tiles_per_seq=tiles_per_seq), nc, rows // tile,
        [pl.BlockSpec((tile, k), lambda j, i: (i, 0)),
         wblk(col0), wblk(col0 + d), wblk(col0 + 2 * d), wblk(gate_col0),
         pl.BlockSpec((8, tc), lambda j, i: (0, j)),
         pl.BlockSpec((8, tc), lambda j, i: (meta_blk, nc + j)),
         pl.BlockSpec((8, tc), lambda j, i: (meta_blk, 2 * nc + j))],
        [pl.BlockSpec((tile, tc), lambda j, i: (i, j))], [jax.ShapeDtypeStruct((rows, d), BF16)],
        [pltpu.VMEM((8, tc), F32)], "conv_proj", (u, w, w, w, w, w8, p_conv_meta, p_conv_meta), rides)
    return out[0] if len(out) == 1 else out


def _pick(n, prefs):
    for p in prefs:
        if n % p == 0:
            return p
    raise ValueError(f"no tile for {n} in {prefs}")


def kernel(x, meta_tokens, norm_mix_g, w_in, rwkv_shift_mu, rwkv_w0, rwkv_w2, rwkv_a0, rwkv_a2, rwkv_k_k, rwkv_k_a, rwkv_r_k, rwkv_ln_w, rwkv_ln_b, conv_w, w_out, norm_mlp_g, w_up, w_down, norm_final_g):
    bsz, seq, d = x.shape
    n_meta = meta_tokens.shape[0]
    depth = w_in.shape[0]
    assert depth == 1, "the meta-token hand-off is written for a single layer"
    assert d % GL == 0 and seq % CHUNK == 0 and n_meta <= META_ROWS
    groups = d // GL
    rows = bsz * seq
    layer = 0

    c_rkv, c_lora = 3 * d, 2 * LORA
    c0 = c_rkv + c_lora
    n_all = w_in.shape[2]
    tc = _pick(d, (256, 128))
    assert c_rkv % c_lora == 0 and n_all == c0 + 5 * d, "unexpected column layout of w_in"
    w_a = _cast_cols(w_in[layer], c0, _pick(d, (128, 64, 32, 16)))
    w2 = rwkv_w2[layer].astype(BF16)
    a2 = rwkv_a2[layer].astype(BF16)

    mu = rwkv_shift_mu[layer]
    mu_rv = jnp.concatenate([mu[:d], mu[2 * d:c_rkv]])
    rows8 = lambda vecs: jnp.zeros((8, d), F32).at[:len(vecs)].set(jnp.stack(vecs))
    key_par = rows8([mu[d:2 * d], rwkv_w0[layer], rwkv_a0[layer], rwkv_k_k[layer], rwkv_k_a[layer]])
    per_group = lambda vec: vec.reshape(groups, 1, GL)
    par = jnp.concatenate(
        [per_group(rwkv_r_k[layer].reshape(d)), per_group(rwkv_ln_w[layer]), per_group(rwkv_ln_b[layer]),
         jnp.zeros((groups, _P_ROWS - 3, GL), F32)], axis=1)
    tw = _pick(d, COL_TILES)
    tk = _pick(d, (512, 256))
    rv_blk = lambda j: j + (j // (d // tw)) * (d // tw)
    k_blk = lambda j: d // tk + j
    lora_blk = lambda j: c_rkv // c_lora + j

    def mixer_inputs(h_rows, batch, tm, before, casts=((), ()), raw=False):
        u = _rmsnorm(h_rows, norm_mix_g[layer], BF16, min(tm, 512))
        z_lora = _shift_proj(u, w_a, lora_blk, mu[c_rkv:c0], before[0], batch, tm, c_lora, raw=raw)
        z_rv = _shift_proj(u, w_a, rv_blk, mu_rv, before[1], batch, tm, tw, casts[0], raw)
        keyed = _key_proj(u, w_a, k_blk, key_par, before[2], z_lora[0] if raw else z_lora, w2, a2, batch, tm, tk,
                          casts[1], raw)
        return u, z_lora, z_rv, keyed

    meta = jnp.zeros((META_ROWS, d), F32).at[META_ROWS - n_meta:].set(meta_tokens.astype(F32))
    nothing = (jnp.zeros((8, c_lora), F32), jnp.zeros((8, 2 * d), F32), jnp.zeros((8, d), F32))
    u_meta, (_, m_lora), (mz_rv, m_rv), m_keyed = mixer_inputs(meta, 1, META_ROWS, nothing, raw=True)
    ng = _pick(groups, (8, 4, 2, 1))
    _, s_meta = _rwkv(mz_rv, m_keyed[:4], jnp.zeros((META_ROWS, d), F32), par, jnp.zeros((groups, GL, GL), F32),
                      1, META_ROWS, ng)

    xf = x.reshape(rows, d)
    tm = _pick(rows, ROW_TILES)
    t_seq = _pick(seq, ROW_TILES)
    n_row = rows // t_seq
    assert bsz % 2 == 0, "the tail of the block alternates between two halves of the batch"
    half = rows // 2
    th = _pick(half, ROW_TILES)
    tn_out = _pick(d, COL_TILES[2:])
    ride = lambda job, steps: [job] if _ride_ok(job["src"], steps) else []
    casts = (ride(_ride_cast(w_in[layer], c0, n_all - c0), 2 * d // tw * n_row),
             ride(_ride_cast(w_out[layer]), d // tk * n_row), ride(_ride_cast(w_up[layer]), d // tc * n_row),
             ride(_ride_cast(w_down[layer]), half // th * (d // tn_out)))
    u, _, z_rv, keyed = mixer_inputs(xf, bsz, t_seq, (m_lora, m_rv, m_keyed[4]), casts[:2])
    z_rv, w_b = z_rv if casts[0] else (z_rv, w_in[layer, :, c0:].astype(BF16))
    keyed, wo = (keyed[:4], keyed[4]) if casts[1] else (keyed, w_out[layer].astype(BF16))
    gate = _matmul(u, w_b, 3 * d, d, tm, _pick(d, COL_TILES), gate=True)
    t_scan = _pick(seq, ROW_TILES[2:])
    ya, _ = _rwkv(z_rv, keyed, gate, par, s_meta.reshape(groups, GL, GL), bsz, t_scan, ng)
    m_conv = _shift_proj(u_meta, w_b, lambda j: j, jnp.zeros((3 * d,), F32), jnp.zeros((8, 3 * d), F32),
                         1, META_ROWS, tc)
    yb = _conv_proj(u, w_b, 0, 4 * d, conv_w[layer], m_conv, bsz, t_seq, tc, casts[2])
    yb, wu = yb if casts[2] else (yb, w_up[layer].astype(BF16))
    g_mlp, g_out = norm_mlp_g[layer], norm_final_g
    tn_up, tk_down = _pick(wu.shape[1], COL_TILES), _pick(wu.shape[1], (4096, 2048) + COL_TILES)
    h1_a = _out_proj(ya, yb, wo, xf, 0, half, th, tn_out, casts[3])
    h1_a, wd = h1_a if casts[3] else (h1_a, w_down[layer].astype(BF16))
    h1_b, u2_a = _out_proj(ya, yb, wo, xf, half, half, th, tn_out, [_ride_norm(h1_a, g_mlp, BF16)])
    hid_a, u2_b = _mlp_up(u2_a, wu, th, tn_up, [_ride_norm(h1_b, g_mlp, BF16)])
    hid_b = _mlp_up(u2_b, wu, th, tn_up)
    h2_a = _mlp_down(hid_a, wd, h1_a, th, _pick(d, COL_TILES), tk_down)
    h2_b = _mlp_down(hid_b, wd, h1_b, th, _pick(d, COL_TILES), tk_down)
    tq = _pick(half, ROW_TILES[1:])
    out = _rmsnorm_into(h2_b, g_out, _rmsnorm_into(h2_a, g_out, rows, 0, tq), half, tq)
    return out.reshape(bsz, seq, d)
```
